```python
import jax
import jax.numpy as jnp
from jax import lax
import numpy as np

D_MODEL = 2048
BATCH = 2
SEQ = 16384
DEPTH = 4

CTX_LEN = 256
GRID_W = 64
HEAD_SIZE = 64
RWKV_WIDTH = 1024
N_HEADS = RWKV_WIDTH // HEAD_SIZE
POOL_WIDTH = D_MODEL - RWKV_WIDTH
MIX_WIDTH = RWKV_WIDTH + POOL_WIDTH
POOL_WINDOWS = (2, 4, 8, 16)
POOL_GROUP = POOL_WIDTH // len(POOL_WINDOWS)
DECAY_LORA = 96
ICLR_LORA = 96
GATE_LORA = 256
CONV_WIDTH = 3
D_FF = -(-(8 * D_MODEL) // (3 * 256)) * 256
N_DIRS = 2
RMS_EPS = 1e-6
GN_EPS = 64e-5

OFF_R = 0
OFF_K = RWKV_WIDTH
OFF_V = 2 * RWKV_WIDTH
OFF_W = 3 * RWKV_WIDTH
OFF_A = OFF_W + DECAY_LORA
OFF_G = OFF_A + ICLR_LORA
OFF_POOL = OFF_G + GATE_LORA
IN_WIDTH = OFF_POOL + POOL_WIDTH

kernel_name = "hymba_rwkv7_poolformer_dit_trunk"


def rms_norm(x, g):
    xf = x.astype(jnp.float32)
    y = xf * lax.rsqrt(jnp.mean(xf * xf, axis=-1, keepdims=True) + RMS_EPS)
    return (y * g.astype(jnp.float32)).astype(x.dtype)


def modulated_norm(x, g, shift, scale):
    return rms_norm(x, g) * (1 + scale) + shift


def ada_modulation(cond, w, b):
    m = jax.nn.silu(cond) @ w + b
    return jnp.split(m, 6, axis=-1)


def swiglu(h, w_gate, w_up, w_down):
    return (jax.nn.silu(h @ w_gate) * (h @ w_up)) @ w_down


def short_conv(u, w):
    up = jnp.pad(u, ((0, 0), (1, 1), (0, 0)))
    return up[:, :-2] * w[0] + up[:, 1:-1] * w[1] + up[:, 2:] * w[2]


def wkv7_scan(r, w, k, v, a_vec, b_vec, s0, reverse):
    def step(s, inp):
        r_t, w_t, k_t, v_t, a_t, b_t = inp
        sa = jnp.einsum('bhvk,bhk->bhv', s, a_t)
        s = s * w_t[:, :, None, :] + sa[..., None] * b_t[:, :, None, :] + v_t[..., None] * k_t[:, :, None, :]
        y = jnp.einsum('bhvk,bhk->bhv', s, r_t)
        return s, y
    xs = tuple(jnp.swapaxes(z, 0, 1) for z in (r, w, k, v, a_vec, b_vec))
    s_fin, ys = lax.scan(step, s0, xs, reverse=reverse)
    return jnp.swapaxes(ys, 0, 1), s_fin


def rwkv_time_mix(u, conv_w, decay_bias, decay_up, iclr_bias, iclr_up, gate_up,
                  k_k, k_a, r_k, gn_w, gn_b, s0_fwd, s0_bwd, need_output):
    f32 = jnp.float32
    b, t, _ = u.shape
    heads = lambda z: z.reshape(b, t, N_HEADS, HEAD_SIZE)
    hvec = lambda p: p.astype(f32).reshape(N_HEADS, HEAD_SIZE)
    rkv = short_conv(u[..., :OFF_W], conv_w).astype(f32)
    r = heads(rkv[..., OFF_R:OFF_K])
    k = heads(rkv[..., OFF_K:OFF_V])
    v = heads(rkv[..., OFF_V:OFF_W])
    w_lo = jnp.tanh(u[..., OFF_W:OFF_A].astype(f32))
    a_lo = u[..., OFF_A:OFF_G].astype(f32)
    kk = k * hvec(k_k)
    kk = kk * lax.rsqrt(jnp.maximum(jnp.sum(kk * kk, axis=-1, keepdims=True), 1e-24))
    ys, ks, states = [], [], []
    for d, s0, rev in ((0, s0_fwd, False), (1, s0_bwd, True)):
        w_log = -jax.nn.softplus(-(decay_bias[d].astype(f32) + w_lo @ decay_up[d].astype(f32))) - 0.5
        decay = heads(jnp.exp(-jnp.exp(w_log)))
        a = heads(jax.nn.sigmoid(iclr_bias[d].astype(f32) + a_lo @ iclr_up[d].astype(f32)))
        k_d = k * (1 + (a - 1) * hvec(k_a))
        y_d, s_d = wkv7_scan(r, decay, k_d, v, -kk, kk * a, s0, rev)
        ys.append(y_d)
        ks.append(k_d)
        states.append(s_d)
    if not need_output:
        return None, states[0], states[1]
    y = ys[0] + ys[1]
    mu = jnp.mean(y, axis=-1, keepdims=True)
    var = jnp.mean(jnp.square(y - mu), axis=-1, keepdims=True)
    y = ((y - mu) * lax.rsqrt(var + GN_EPS)) * hvec(gn_w) + hvec(gn_b)
    bonus = jnp.sum(r * (ks[0] + ks[1]) * hvec(r_k), axis=-1, keepdims=True) * v
    g = jax.nn.sigmoid(u[..., OFF_G:OFF_POOL].astype(f32)) @ gate_up.astype(f32)
    out = (y + bonus).reshape(b, t, RWKV_WIDTH) * g
    return out.astype(u.dtype), states[0], states[1]


def window_bounds(n, window):
    i = jnp.arange(n)
    half = window // 2
    return jnp.clip(i - half, 0, n), jnp.clip(i + half, 0, n)


def box_mean_1d(x, window):
    t = x.shape[1]
    cs = jnp.pad(jnp.cumsum(x, axis=1), ((0, 0), (1, 0), (0, 0)))
    lo, hi = window_bounds(t, window)
    cnt = (hi - lo).astype(x.dtype)
    return (cs[:, hi] - cs[:, lo]) / cnt[None, :, None]


def box_mean_2d(x, window):
    rows, cols = x.shape[1], x.shape[2]
    sat = jnp.pad(jnp.cumsum(jnp.cumsum(x, axis=1), axis=2), ((0, 0), (1, 0), (1, 0), (0, 0)))
    r_lo, r_hi = window_bounds(rows, window)
    c_lo, c_hi = window_bounds(cols, window)
    top = sat[:, r_lo]
    bot = sat[:, r_hi]
    total = bot[:, :, c_hi] - bot[:, :, c_lo] - top[:, :, c_hi] + top[:, :, c_lo]
    cnt = ((r_hi - r_lo)[:, None] * (c_hi - c_lo)[None, :]).astype(x.dtype)
    return total / cnt[None, :, :, None]


def pool_mix(u, pool_w, pool_scale, on_grid):
    b, t, _ = u.shape
    uf = u.astype(jnp.float32)
    outs = []
    for gi, win in enumerate(POOL_WINDOWS):
        ug = uf[..., gi * POOL_GROUP:(gi + 1) * POOL_GROUP]
        if on_grid:
            rows = t // GRID_W
            m = box_mean_2d(ug.reshape(b, rows, GRID_W, POOL_GROUP), win).reshape(b, t, POOL_GROUP)
        else:
            m = box_mean_1d(ug, win)
        outs.append((m - ug) @ pool_w[gi].astype(jnp.float32))
    return (jnp.concatenate(outs, axis=-1) * pool_scale.astype(jnp.float32)).astype(u.dtype)


def setup_inputs(seed: int = 0) -> dict:
    key = jax.random.key(seed)
    ks = jax.random.split(key, 32)
    f32 = jnp.float32
    nrm = lambda k, shape, s: s * jax.random.normal(k, shape, f32)
    L, RW = DEPTH, RWKV_WIDTH
    conv_center = jnp.zeros((CONV_WIDTH, 1), f32).at[CONV_WIDTH // 2].set(1.0)
    return {
        'x': nrm(ks[0], (BATCH, SEQ, D_MODEL), 1.0),
        'c': nrm(ks[1], (BATCH, D_MODEL), 1.0),
        'ctx': nrm(ks[2], (BATCH, CTX_LEN, D_MODEL), 1.0),
        'c_ctx': nrm(ks[3], (D_MODEL,), 1.0),
        'ada_w': nrm(ks[4], (L, D_MODEL, 6 * D_MODEL), 0.5 * D_MODEL ** -0.5),
        'ada_b': nrm(ks[5], (L, 6 * D_MODEL), 0.02),
        'norm_mix': 1.0 + nrm(ks[6], (L, D_MODEL), 0.02),
        'norm_ffn': 1.0 + nrm(ks[7], (L, D_MODEL), 0.02),
        'w_in': nrm(ks[8], (L, D_MODEL, IN_WIDTH), D_MODEL ** -0.5),
        'conv_rkv': conv_center[None] + nrm(ks[9], (L, CONV_WIDTH, 3 * RW), 0.3),
        'decay_bias': jax.random.uniform(ks[10], (L, N_DIRS, RW), f32, -4.0, -0.5),
        'decay_up': nrm(ks[11], (L, N_DIRS, DECAY_LORA, RW), 0.1 * DECAY_LORA ** -0.5),
        'iclr_bias': nrm(ks[12], (L, N_DIRS, RW), 0.1),
        'iclr_up': nrm(ks[13], (L, N_DIRS, ICLR_LORA, RW), 0.3 * ICLR_LORA ** -0.5),
        'gate_up': nrm(ks[14], (L, GATE_LORA, RW), GATE_LORA ** -0.5),
        'k_k': 0.85 + nrm(ks[15], (L, RW), 0.05),
        'k_a': 1.0 + nrm(ks[16], (L, RW), 0.05),
        'r_k': nrm(ks[17], (L, RW), 0.1),
        'gn_w': 1.0 + nrm(ks[18], (L, RW), 0.05),
        'gn_b': nrm(ks[19], (L, RW), 0.02),
        'pool_w': nrm(ks[20], (L, len(POOL_WINDOWS), POOL_GROUP, POOL_GROUP), POOL_GROUP ** -0.5),
        'pool_scale': 1.0 + nrm(ks[21], (L, POOL_WIDTH), 0.1),
        'w_out': nrm(ks[22], (L, MIX_WIDTH, D_MODEL), MIX_WIDTH ** -0.5),
        'ffn_gate': nrm(ks[23], (L, D_MODEL, D_FF), D_MODEL ** -0.5),
        'ffn_up': nrm(ks[24], (L, D_MODEL, D_FF), D_MODEL ** -0.5),
        'ffn_down': nrm(ks[25], (L, D_FF, D_MODEL), D_FF ** -0.5),
        'final_norm': 1.0 + nrm(ks[26], (D_MODEL,), 0.02),
    }


def reference(x, c, ctx, c_ctx, ada_w, ada_b, norm_mix, norm_ffn, w_in, conv_rkv,
              decay_bias, decay_up, iclr_bias, iclr_up, gate_up, k_k, k_a, r_k,
              gn_w, gn_b, pool_w, pool_scale, w_out, ffn_gate, ffn_up, ffn_down, final_norm):
    zero_state = jnp.zeros((x.shape[0], N_HEADS, HEAD_SIZE, HEAD_SIZE), jnp.float32)
    ctx_h = ctx
    for l in range(DEPTH):
        last = l == DEPTH - 1
        rw = (conv_rkv[l], decay_bias[l], decay_up[l], iclr_bias[l], iclr_up[l], gate_up[l],
              k_k[l], k_a[l], r_k[l], gn_w[l], gn_b[l])
        sh1, sc1, gt1, sh2, sc2, gt2 = ada_modulation(c[:, None, :], ada_w[l], ada_b[l])
        csh1, csc1, cgt1, csh2, csc2, cgt2 = ada_modulation(c_ctx, ada_w[l], ada_b[l])

        u_ctx = modulated_norm(ctx_h, norm_mix[l], csh1, csc1) @ w_in[l]
        y_ctx, s_fwd, s_bwd = rwkv_time_mix(u_ctx[..., :OFF_POOL], *rw, zero_state, zero_state, not last)

        u = modulated_norm(x, norm_mix[l], sh1, sc1) @ w_in[l]
        y_lat, _, _ = rwkv_time_mix(u[..., :OFF_POOL], *rw, s_fwd, s_bwd, True)
        p_lat = pool_mix(u[..., OFF_POOL:], pool_w[l], pool_scale[l], True)
        x = x + gt1 * (jnp.concatenate([y_lat, p_lat], axis=-1) @ w_out[l])
        x = x + gt2 * swiglu(modulated_norm(x, norm_ffn[l], sh2, sc2), ffn_gate[l], ffn_up[l], ffn_down[l])

        if not last:
            p_ctx = pool_mix(u_ctx[..., OFF_POOL:], pool_w[l], pool_scale[l], False)
            ctx_h = ctx_h + cgt1 * (jnp.concatenate([y_ctx, p_ctx], axis=-1) @ w_out[l])
            ctx_h = ctx_h + cgt2 * swiglu(modulated_norm(ctx_h, norm_ffn[l], csh2, csc2),
                                          ffn_gate[l], ffn_up[l], ffn_down[l])
    return rms_norm(x, final_norm)
```

```python
import functools

import numpy as np
import jax
import jax.numpy as jnp
from jax import lax
from jax.experimental import pallas as pl
from jax.experimental.pallas import tpu as pltpu

F32 = jnp.float32
BF16 = jnp.bfloat16

HEAD = 64
LANES = 128
GRID_COLS = 64
POOL_WINS = (2, 4, 8, 16)
POOL_HALO_ROWS = 8
RMS_EPS = 1e-6
GN_EPS = 64e-5
DECAY_LORA = 96
ICLR_LORA = 96
GATE_LORA = 256
LORA_PAD = 128
VMEM_LIMIT = 56 * 1024 * 1024


def _cparams(sem):
    return pltpu.CompilerParams(dimension_semantics=sem, vmem_limit_bytes=VMEM_LIMIT)


def _dot(a, b):
    return jnp.dot(a, b, preferred_element_type=F32)


def _dot_nt(a, b):
    return lax.dot_general(a, b, (((1,), (1,)), ((), ())), preferred_element_type=F32)


def _dot_tn(a, b):
    return lax.dot_general(a, b, (((0,), (0,)), ((), ())), preferred_element_type=F32)


def _split(x):
    hi = x.astype(BF16)
    lo = (x - hi.astype(F32)).astype(BF16)
    return hi, lo


def _head_sum(x, ones_bd):
    hi, lo = _split(x)
    parts = []
    for c in range(x.shape[1] // LANES):
        sl = slice(c * LANES, (c + 1) * LANES)
        parts.append(_dot(hi[:, sl], ones_bd) + _dot(lo[:, sl], ones_bd))
    return parts[0] if len(parts) == 1 else jnp.concatenate(parts, axis=1)


def _sigmoid(x):
    return 1.0 / (1.0 + jnp.exp(-x))


def _silu(x):
    return x * _sigmoid(x)


def _ada_kernel(c_ref, w_ref, b_ref, o_ref):
    c = c_ref[...]
    o_ref[0] = _dot(_silu(c).astype(BF16), w_ref[0].astype(BF16)) + b_ref[0]


def _ada_call(cond, ada_w, ada_b, tn):
    L, D, N = ada_w.shape
    R = cond.shape[0]
    return pl.pallas_call(
        _ada_kernel,
        grid=(L, N // tn),
        in_specs=[
            pl.BlockSpec((R, D), lambda l, j: (0, 0)),
            pl.BlockSpec((1, D, tn), lambda l, j: (l, 0, j)),
            pl.BlockSpec((1, 1, tn), lambda l, j: (l, 0, j)),
        ],
        out_specs=pl.BlockSpec((1, R, tn), lambda l, j: (l, 0, j)),
        out_shape=jax.ShapeDtypeStruct((L, R, N), F32),
        compiler_params=_cparams(("parallel", "parallel")),
        name="ada_modulation",
    )(cond, ada_w, ada_b.reshape(L, 1, N))


def _modnorm(x, g, shift, scale):
    ms = jnp.mean(x * x, axis=-1, keepdims=True)
    return (x * lax.rsqrt(ms + RMS_EPS) * g) * (1.0 + scale) + shift


def _inproj_kernel(x_ref, g_ref, sh_ref, sc_ref, w_ref, o_ref, xn_ref):
    @pl.when(pl.program_id(2) == 0)
    def _():
        xn_ref[...] = _modnorm(x_ref[0], g_ref[...], sh_ref[0], sc_ref[0]).astype(BF16)

    o_ref[0] = _dot(xn_ref[...], w_ref[...])


def _inproj_call(x, g, shift, scale, w, tm, tn):
    B, T, D = x.shape
    N = w.shape[1]
    tm = min(tm, T)
    return pl.pallas_call(
        _inproj_kernel,
        grid=(B, T // tm, N // tn),
        in_specs=[
            pl.BlockSpec((1, tm, D), lambda b, i, j: (b, i, 0)),
            pl.BlockSpec((1, D), lambda b, i, j: (0, 0)),
            pl.BlockSpec((1, 1, D), lambda b, i, j: (b, 0, 0)),
            pl.BlockSpec((1, 1, D), lambda b, i, j: (b, 0, 0)),
            pl.BlockSpec((D, tn), lambda b, i, j: (0, j)),
        ],
        out_specs=pl.BlockSpec((1, tm, tn), lambda b, i, j: (b, i, j)),
        out_shape=jax.ShapeDtypeStruct((B, T, N), F32),
        scratch_shapes=[pltpu.VMEM((tm, D), BF16)],
        compiler_params=_cparams(("parallel", "parallel", "arbitrary")),
        name="norm_inproj",
    )(x, g, shift, scale, w)


def _conv3(x, prev8, next8, w, has_prev, has_next):
    tt = x.shape[0]
    row = lax.broadcasted_iota(jnp.int32, x.shape, 0)
    up = jnp.where(row == 0, prev8[7:8, :] * has_prev, pltpu.roll(x, 1, 0))
    dn = jnp.where(row == tt - 1, next8[0:1, :] * has_next, pltpu.roll(x, tt - 1, 0))
    return up * w[0:1, :] + x * w[1:2, :] + dn * w[2:3, :]


def _prep_kernel(ur, uk, uv, pr, pk, pv, nr, nk, nv, lora, cw, dbias, dup, ibias, iup, gup,
                 kkw, ka, rk, ones_ref,
                 r_o, k_o, v_o, kk_o, a0_o, a1_o, lw0_o, lw1_o, g_o, bonus_o):
    i = pl.program_id(1)
    has_prev = (i > 0).astype(F32)
    has_next = (i < pl.num_programs(1) - 1).astype(F32)
    W = r_o.shape[2]
    ones_bd = ones_ref[...]
    r = _conv3(ur[0], pr[0], nr[0], cw[:, 0:W], has_prev, has_next)
    k = _conv3(uk[0], pk[0], nk[0], cw[:, W:2 * W], has_prev, has_next)
    v = _conv3(uv[0], pv[0], nv[0], cw[:, 2 * W:3 * W], has_prev, has_next)
    r_o[0] = r
    k_o[0] = k
    v_o[0] = v
    kk = k * kkw[...]
    kk = kk * lax.rsqrt(jnp.maximum(_head_sum(kk * kk, ones_bd), 1e-24))
    kk_o[0] = kk
    lo = lora[0]
    w_lo = jnp.tanh(lo[:, 0:LORA_PAD]).astype(BF16)
    a_lo = lo[:, LORA_PAD:2 * LORA_PAD].astype(BF16)
    g_lo = _sigmoid(lo[:, 2 * LORA_PAD:2 * LORA_PAD + GATE_LORA]).astype(BF16)
    a_sum = None
    for d, (a_o, lw_o) in enumerate(((a0_o, lw0_o), (a1_o, lw1_o))):
        z = dbias[d:d + 1, :] + _dot(w_lo, dup[d])
        w_log = jnp.minimum(z, 0.0) - jnp.log(1.0 + jnp.exp(-jnp.abs(z))) - 0.5
        lw_o[0] = -jnp.exp(w_log)
        a = _sigmoid(ibias[d:d + 1, :] + _dot(a_lo, iup[d]))
        a_o[0] = a
        a_sum = a if a_sum is None else a_sum + a
    g_o[0] = _dot(g_lo, gup[...])
    kpair = k * (2.0 + (a_sum - 2.0) * ka[...])
    bonus_o[0] = _head_sum(r * kpair * rk[...], ones_bd) * v


def _prep_call(u, p, tt):
    B, T, _ = u.shape
    W = p["k_k"].shape[1]
    tt = min(tt, T)
    nb8 = T // 8
    cur = lambda c: pl.BlockSpec((1, tt, W), lambda b, i: (b, i, c))
    prv = lambda c: pl.BlockSpec((1, 8, W), lambda b, i: (b, jnp.maximum(i * (tt // 8) - 1, 0), c))
    nxt = lambda c: pl.BlockSpec((1, 8, W), lambda b, i: (b, jnp.minimum((i + 1) * (tt // 8), nb8 - 1), c))
    full = lambda a: pl.BlockSpec(a.shape, lambda b, i: (0,) * a.ndim)
    lora_blk = (u.shape[2] - 2 * LORA_PAD - GATE_LORA) // (2 * LORA_PAD + GATE_LORA)
    consts = [p["conv"], p["dbias"], p["dup"], p["ibias"], p["iup"], p["gup"],
              p["k_k"], p["k_a"], p["r_k"], p["ones_bd"]]
    out_blk = pl.BlockSpec((1, tt, W), lambda b, i: (b, i, 0))
    return pl.pallas_call(
        _prep_kernel,
        grid=(B, T // tt),
        in_specs=[cur(0), cur(1), cur(2), prv(0), prv(1), prv(2), nxt(0), nxt(1), nxt(2),
                  pl.BlockSpec((1, tt, 2 * LORA_PAD + GATE_LORA), lambda b, i: (b, i, lora_blk))]
                 + [full(a) for a in consts],
        out_specs=[out_blk] * 10,
        out_shape=[jax.ShapeDtypeStruct((B, T, W), F32)] * 10,
        compiler_params=_cparams(("parallel", "parallel")),
        name="rwkv_prep",
    )(u, u, u, u, u, u, u, u, u, u, *consts)


def _scan_masks(C):
    P = 2 * C
    p = np.arange(P)[:, None]
    q = np.arange(P)[None, :]
    same = (p // C) == (q // C)
    t, j = p % C, q % C
    fwd = [same & (t > j), same & (t >= j)]
    rev = [same & (t < j), same & (t <= j)]
    s = 1
    while s < C:
        blk = (p // (2 * s)) == (q // (2 * s))
        fwd.append(blk & ((p % (2 * s)) >= s) & ((q % (2 * s)) < s))
        rev.append(blk & ((p % (2 * s)) < s) & ((q % (2 * s)) >= s))
        s *= 2
    return np.stack(fwd + rev).astype(np.float32)


def _chunk_pair(At, Rt, Bt, Kt, v, gam, Z, masks, mbase, nlev, m0, m1, eye):
    C = At.shape[0]
    P = 2 * C
    stack = lambda x: jnp.concatenate([x * m0, x * m1], axis=0)
    Am, Rm, Bm, Km, Vs = stack(At), stack(Rt), stack(Bt), stack(Kt), stack(v)
    AR = jnp.concatenate([Am, Rm], axis=0).astype(BF16)
    Bmb, Kmb, Vsb = Bm.astype(BF16), Km.astype(BF16), Vs.astype(BF16)
    scb = _dot_nt(AR, Bmb)
    sck = _dot_nt(AR, Kmb)
    x1 = _dot_nt(AR, Z.astype(BF16))
    strict = masks[mbase]
    incl = masks[mbase + 1]
    Nab = scb[:P] * strict
    Nak = sck[:P] * strict
    Prb = scb[P:] * incl
    Prk = sck[P:] * incl
    T = eye + Nab * masks[mbase + 2]
    for lv in range(1, nlev):
        Tb = T.astype(BF16)
        Nl = (Nab * masks[mbase + 2 + lv]).astype(BF16)
        T = T + _dot(_dot(Tb, Nl).astype(BF16), Tb)
    Us = _dot(T.astype(BF16), (x1[:P] + _dot(Nak.astype(BF16), Vsb)).astype(BF16))
    UV = jnp.concatenate([Us.astype(BF16), Vsb], axis=0)
    Ys = x1[P:] + _dot(jnp.concatenate([Prb, Prk], axis=1).astype(BF16), UV)
    Zn = gam * (Z + _dot_tn(UV, jnp.concatenate([Bmb, Kmb], axis=0)))
    return Ys[:C] + Ys[C:], Zn


def _scan_kernel(rf, kf, vf, kkf, af, lwf, rb, kb, vb, kkb, ab, lwb, s0f, s0b, ka_ref, tri_ref, m_ref,
                 yf_o, yb_o, sf_o, sb_o, *, C, G, nchunk):
    @pl.when(pl.program_id(2) == 0)
    def _():
        sf_o[...] = s0f[...]
        sb_o[...] = s0b[...]

    nlev = int(np.log2(C))
    nmask = 2 + nlev
    lane = lax.broadcasted_iota(jnp.int32, (1, LANES), 1)
    m0 = (lane < HEAD).astype(F32)
    m1 = 1.0 - m0
    ri = lax.broadcasted_iota(jnp.int32, (2 * C, 2 * C), 0)
    ci = lax.broadcasted_iota(jnp.int32, (2 * C, 2 * C), 1)
    eye = (ri == ci).astype(F32)
    ka = ka_ref[...]
    dirs = ((rf, kf, vf, kkf, af, lwf, yf_o, sf_o, False), (rb, kb, vb, kkb, ab, lwb, yb_o, sb_o, True))
    for cidx in range(nchunk):
        for (r_r, k_r, v_r, kk_r, a_r, lw_r, y_o, s_o, rev) in dirs:
            c = (nchunk - 1 - cidx) if rev else cidx
            rows = slice(c * C, (c + 1) * C)
            r, k, v, kk = r_r[0, rows, :], k_r[0, rows, :], v_r[0, rows, :], kk_r[0, rows, :]
            a, lw = a_r[0, rows, :], lw_r[0, rows, :]
            hi, lo = _split(lw)
            tri = tri_ref[1 if rev else 0]
            cum = _dot(tri, hi) + _dot(tri, lo)
            ea = jnp.exp(cum - lw)
            er = jnp.exp(cum)
            einv = jnp.exp(-cum)
            At = -(kk * ea)
            Rt = r * er
            Bt = (kk * a) * einv
            Kt = (k * (1.0 + (a - 1.0) * ka)) * einv
            gam = er[0:1, :] if rev else er[C - 1:C, :]
            ys = []
            for g in range(G):
                sl = slice(g * LANES, (g + 1) * LANES)
                y, zn = _chunk_pair(At[:, sl], Rt[:, sl], Bt[:, sl], Kt[:, sl], v[:, sl], gam[:, sl],
                                    s_o[0, g], m_ref, nmask if rev else 0, nlev, m0, m1, eye)
                s_o[0, g] = zn
                ys.append(y)
            y_o[0, rows, :] = ys[0] if G == 1 else jnp.concatenate(ys, axis=1)


def _scan_call(pre, a0, a1, lw0, lw1, s0f, s0b, ka, C, G, TB):
    r, k, v, kk = pre
    B, T, W = r.shape
    TB = min(TB, T)
    NT = T // TB
    NG = W // (G * LANES)
    GW = G * LANES
    fwd = pl.BlockSpec((1, TB, GW), lambda b, g, i: (b, i, g))
    bwd = pl.BlockSpec((1, TB, GW), lambda b, g, i: (b, NT - 1 - i, g))
    st = pl.BlockSpec((1, G, LANES, LANES), lambda b, g, i: (b, g, 0, 0))
    masks = jnp.asarray(_scan_masks(C))
    tri = jnp.asarray(np.stack([np.tril(np.ones((C, C), np.float32)),
                                np.triu(np.ones((C, C), np.float32))])).astype(BF16)
    kern = functools.partial(_scan_kernel, C=C, G=G, nchunk=TB // C)
    return pl.pallas_call(
        kern,
        grid=(B, NG, NT),
        in_specs=[fwd] * 6 + [bwd] * 6 + [st, st,
                  pl.BlockSpec((1, GW), lambda b, g, i: (0, g)),
                  pl.BlockSpec(tri.shape, lambda b, g, i: (0, 0, 0)),
                  pl.BlockSpec(masks.shape, lambda b, g, i: (0, 0, 0))],
        out_specs=[fwd, bwd, st, st],
        out_shape=[jax.ShapeDtypeStruct((B, T, W), F32)] * 2
                  + [jax.ShapeDtypeStruct(s0f.shape, F32)] * 2,
        compiler_params=_cparams(("parallel", "parallel", "arbitrary")),
        name="wkv7_scan",
    )(r, k, v, kk, a0, lw0, r, k, v, kk, a1, lw1, s0f, s0b, ka, tri, masks)


def _post_kernel(yf, yb, bonus, g, gnw, gnb, ones_ref, o_ref):
    ones_bd = ones_ref[...]
    y = yf[0] + yb[0]
    mu = _head_sum(y, ones_bd) * (1.0 / HEAD)
    yc = y - mu
    var = _head_sum(yc * yc, ones_bd) * (1.0 / HEAD)
    yn = yc * lax.rsqrt(var + GN_EPS) * gnw[...] + gnb[...]
    o_ref[0] = (yn + bonus[0]) * g[0]


def _post_call(yf, yb, bonus, g, gnw, gnb, ones_bd, tt):
    B, T, W = yf.shape
    tt = min(tt, T)
    blk = pl.BlockSpec((1, tt, W), lambda b, i: (b, i, 0))
    vec = pl.BlockSpec((1, W), lambda b, i: (0, 0))
    return pl.pallas_call(
        _post_kernel,
        grid=(B, T // tt),
        in_specs=[blk, blk, blk, blk, vec, vec, pl.BlockSpec(ones_bd.shape, lambda b, i: (0, 0))],
        out_specs=blk,
        out_shape=jax.ShapeDtypeStruct((B, T, W), F32),
        compiler_params=_cparams(("parallel", "parallel")),
        name="rwkv_post",
    )(yf, yb, bonus, g, gnw, gnb, ones_bd)


def _window_count(pos, half, n):
    return (jnp.minimum(pos + half, n) - jnp.maximum(pos - half, 0)).astype(F32)


def _pool2d_kernel(cur, prv, nxt, pw_ref, ps_ref, o_ref, col_ref, *, tp, halo, nrows):
    i = pl.program_id(1)
    has_prev = (i > 0).astype(F32)
    has_next = (i < pl.num_programs(1) - 1).astype(F32)
    PG = pw_ref.shape[1]
    shift = GRID_COLS.bit_length() - 1
    tok = lax.broadcasted_iota(jnp.int32, (tp, PG), 0)
    col = jnp.bitwise_and(tok, GRID_COLS - 1)
    row = i * (tp // GRID_COLS) + jnp.right_shift(tok, shift)
    p = lax.broadcasted_iota(jnp.int32, (LANES, LANES), 0)
    q = lax.broadcasted_iota(jnp.int32, (LANES, LANES), 1)
    same_row = jnp.right_shift(p, shift) == jnp.right_shift(q, shift)
    dcol = jnp.bitwise_and(p, GRID_COLS - 1) - jnp.bitwise_and(q, GRID_COLS - 1)
    for gi, win in enumerate(POOL_WINS):
        half = win // 2
        cs = slice(gi * PG, (gi + 1) * PG)
        band = (same_row & (dcol <= half) & (dcol > -half)).astype(BF16)

        def colsum(x):
            hi, lo = _split(x)
            return _dot(band, hi) + _dot(band, lo)

        for c in range(halo // LANES):
            rs = slice(c * LANES, (c + 1) * LANES)
            col_ref[rs, :] = colsum(prv[0, rs, cs] * has_prev)
            rs2 = slice(halo + tp + c * LANES, halo + tp + (c + 1) * LANES)
            col_ref[rs2, :] = colsum(nxt[0, rs, cs] * has_next)
        for c in range(tp // LANES):
            rs = slice(c * LANES, (c + 1) * LANES)
            col_ref[halo + c * LANES:halo + (c + 1) * LANES, :] = colsum(cur[0, rs, cs])
        acc = None
        for dr in range(-half, half):
            part = col_ref[halo + dr * GRID_COLS:halo + dr * GRID_COLS + tp, :]
            acc = part if acc is None else acc + part
        cnt = _window_count(col, half, GRID_COLS) * _window_count(row, half, nrows)
        diff = acc * (1.0 / cnt) - cur[0, :, cs]
        o_ref[0, :, cs] = _dot(diff.astype(BF16), pw_ref[gi]) * ps_ref[:, cs]


def _pool2d_call(u, pool_w, pool_scale, tp):
    B, T, _ = u.shape
    PW = pool_scale.shape[1]
    halo = POOL_HALO_ROWS * GRID_COLS
    tp = min(tp, T)
    cblk = (u.shape[2] - 2 * LORA_PAD - GATE_LORA - PW) // PW
    nh = T // halo
    kern = functools.partial(_pool2d_kernel, tp=tp, halo=halo, nrows=T // GRID_COLS)
    return pl.pallas_call(
        kern,
        grid=(B, T // tp),
        in_specs=[
            pl.BlockSpec((1, tp, PW), lambda b, i: (b, i, cblk)),
            pl.BlockSpec((1, halo, PW), lambda b, i: (b, jnp.maximum(i * (tp // halo) - 1, 0), cblk)),
            pl.BlockSpec((1, halo, PW), lambda b, i: (b, jnp.minimum((i + 1) * (tp // halo), nh - 1), cblk)),
            pl.BlockSpec(pool_w.shape, lambda b, i: (0, 0, 0)),
            pl.BlockSpec((1, PW), lambda b, i: (0, 0)),
        ],
        out_specs=pl.BlockSpec((1, tp, PW), lambda b, i: (b, i, 0)),
        out_shape=jax.ShapeDtypeStruct((B, T, PW), F32),
        scratch_shapes=[pltpu.VMEM((tp + 2 * halo, pool_w.shape[1]), F32)],
        compiler_params=_cparams(("parallel", "parallel")),
        name="pool2d",
    )(u, u, u, pool_w, pool_scale)


def _pool1d_kernel(u_ref, pw_ref, ps_ref, o_ref):
    T = u_ref.shape[1]
    PG = pw_ref.shape[1]
    p = lax.broadcasted_iota(jnp.int32, (T, T), 0)
    q = lax.broadcasted_iota(jnp.int32, (T, T), 1)
    pos = lax.broadcasted_iota(jnp.int32, (T, PG), 0)
    for gi, win in enumerate(POOL_WINS):
        half = win // 2
        cs = slice(gi * PG, (gi + 1) * PG)
        band = ((p - q <= half) & (p - q > -half)).astype(BF16)
        x = u_ref[0, :, cs]
        hi, lo = _split(x)
        m = (_dot(band, hi) + _dot(band, lo)) * (1.0 / _window_count(pos, half, T))
        o_ref[0, :, cs] = _dot((m - x).astype(BF16), pw_ref[gi]) * ps_ref[:, cs]


def _pool1d_call(u, pool_w, pool_scale):
    B, T, _ = u.shape
    PW = pool_scale.shape[1]
    cblk = (u.shape[2] - 2 * LORA_PAD - GATE_LORA - PW) // PW
    return pl.pallas_call(
        _pool1d_kernel,
        grid=(B,),
        in_specs=[
            pl.BlockSpec((1, T, PW), lambda b: (b, 0, cblk)),
            pl.BlockSpec(pool_w.shape, lambda b: (0, 0, 0)),
            pl.BlockSpec((1, PW), lambda b: (0, 0)),
        ],
        out_specs=pl.BlockSpec((1, T, PW), lambda b: (b, 0, 0)),
        out_shape=jax.ShapeDtypeStruct((B, T, PW), F32),
        compiler_params=_cparams(("parallel",)),
        name="pool1d",
    )(u, pool_w, pool_scale)


def _outproj_kernel(x_ref, y_ref, p_ref, gt_ref, w_ref, o_ref):
    W = y_ref.shape[2]
    acc = _dot(y_ref[0].astype(BF16), w_ref[0:W, :]) + _dot(p_ref[0].astype(BF16), w_ref[W:, :])
    o_ref[0] = x_ref[0] + gt_ref[0] * acc


def _outproj_call(x, y, p, gate, w, tm):
    B, T, D = x.shape
    W, PW = y.shape[2], p.shape[2]
    tm = min(tm, T)
    return pl.pallas_call(
        _outproj_kernel,
        grid=(B, T // tm),
        in_specs=[
            pl.BlockSpec((1, tm, D), lambda b, i: (b, i, 0)),
            pl.BlockSpec((1, tm, W), lambda b, i: (b, i, 0)),
            pl.BlockSpec((1, tm, PW), lambda b, i: (b, i, 0)),
            pl.BlockSpec((1, 1, D), lambda b, i: (b, 0, 0)),
            pl.BlockSpec(w.shape, lambda b, i: (0, 0)),
        ],
        out_specs=pl.BlockSpec((1, tm, D), lambda b, i: (b, i, 0)),
        out_shape=jax.ShapeDtypeStruct((B, T, D), F32),
        compiler_params=_cparams(("parallel", "parallel")),
        name="outproj_residual",
    )(x, y, p, gate, w)


def _ffn_kernel(x_ref, g_ref, sh_ref, sc_ref, gt_ref, wg_ref, wu_ref, wd_ref, o_ref, xn_ref, acc_ref):
    j = pl.program_id(2)

    @pl.when(j == 0)
    def _():
        xn_ref[...] = _modnorm(x_ref[0], g_ref[...], sh_ref[0], sc_ref[0]).astype(BF16)
        acc_ref[...] = jnp.zeros_like(acc_ref)

    xn = xn_ref[...]
    h = _silu(_dot(xn, wg_ref[...])) * _dot(xn, wu_ref[...])
    acc_ref[...] += _dot(h.astype(BF16), wd_ref[...])

    @pl.when(j == pl.num_programs(2) - 1)
    def _():
        o_ref[0] = x_ref[0] + gt_ref[0] * acc_ref[...]


def _ffn_call(x, g, shift, scale, gate, wg, wu, wd, tm, tf):
    B, T, D = x.shape
    F = wg.shape[1]
    tm = min(tm, T)
    mod = pl.BlockSpec((1, 1, D), lambda b, i, j: (b, 0, 0))
    return pl.pallas_call(
        _ffn_kernel,
        grid=(B, T // tm, F // tf),
        in_specs=[
            pl.BlockSpec((1, tm, D), lambda b, i, j: (b, i, 0)),
            pl.BlockSpec((1, D), lambda b, i, j: (0, 0)),
            mod, mod, mod,
            pl.BlockSpec((D, tf), lambda b, i, j: (0, j)),
            pl.BlockSpec((D, tf), lambda b, i, j: (0, j)),
            pl.BlockSpec((tf, D), lambda b, i, j: (j, 0)),
        ],
        out_specs=pl.BlockSpec((1, tm, D), lambda b, i, j: (b, i, 0)),
        out_shape=jax.ShapeDtypeStruct((B, T, D), F32),
        scratch_shapes=[pltpu.VMEM((tm, D), BF16), pltpu.VMEM((tm, D), F32)],
        compiler_params=_cparams(("parallel", "parallel", "arbitrary")),
        name="norm_swiglu_residual",
    )(x, g, shift, scale, gate, wg, wu, wd)


def _rms_kernel(x_ref, g_ref, o_ref):
    x = x_ref[0]
    ms = jnp.mean(x * x, axis=-1, keepdims=True)
    o_ref[0] = x * lax.rsqrt(ms + RMS_EPS) * g_ref[...]


def _rms_call(x, g, tm):
    B, T, D = x.shape
    tm = min(tm, T)
    blk = pl.BlockSpec((1, tm, D), lambda b, i: (b, i, 0))
    return pl.pallas_call(
        _rms_kernel,
        grid=(B, T // tm),
        in_specs=[blk, pl.BlockSpec((1, D), lambda b, i: (0, 0))],
        out_specs=blk,
        out_shape=jax.ShapeDtypeStruct((B, T, D), F32),
        compiler_params=_cparams(("parallel", "parallel")),
        name="final_rmsnorm",
    )(x, g)


def _pad_rows(w, n):
    return jnp.pad(w, ((0, 0),) * (w.ndim - 2) + ((0, n - w.shape[-2]), (0, 0)))


def _layer_params(l, w_in, conv_rkv, decay_bias, decay_up, iclr_bias, iclr_up, gate_up, k_k, k_a, r_k,
                  gn_w, gn_b, pool_w, pool_scale, w_out, ffn_gate, ffn_up, ffn_down):
    RW = k_k.shape[1]
    wl = w_in[l]
    o_w = 3 * RW
    o_a = o_w + DECAY_LORA
    o_g = o_a + ICLR_LORA
    o_p = o_g + GATE_LORA
    padc = lambda w: jnp.pad(w, ((0, 0), (0, LORA_PAD - w.shape[1])))
    w_in_r = jnp.concatenate(
        [wl[:, :o_w], wl[:, o_p:], padc(wl[:, o_w:o_a]), padc(wl[:, o_a:o_g]), wl[:, o_g:o_p]], axis=1)
    row = lambda a: a[l].reshape(1, -1)
    return {
        "w_in": w_in_r.astype(BF16),
        "conv": conv_rkv[l],
        "dbias": decay_bias[l], "dup": _pad_rows(decay_up[l], LORA_PAD).astype(BF16),
        "ibias": iclr_bias[l], "iup": _pad_rows(iclr_up[l], LORA_PAD).astype(BF16),
        "gup": gate_up[l].astype(BF16),
        "k_k": row(k_k), "k_a": row(k_a), "r_k": row(r_k), "gn_w": row(gn_w), "gn_b": row(gn_b),
        "pool_w": pool_w[l].astype(BF16), "pool_scale": row(pool_scale),
        "w_out": w_out[l].astype(BF16),
        "wg": ffn_gate[l].astype(BF16), "wu": ffn_up[l].astype(BF16), "wd": ffn_down[l].astype(BF16),
    }


_CFG = dict(tm_in=512, tn_in=1536, tt_prep=256, scan_c=64, scan_g=8, scan_tb=64, tt_post=512,
            tp_pool=2048, tm_out=512, tm_ffn=512, tf_ffn=512, tm_rms=512, tn_ada=1024)


def _forward(cfg, x, c, ctx, c_ctx, ada_w, ada_b, norm_mix, norm_ffn, w_in, conv_rkv, decay_bias, decay_up,
             iclr_bias, iclr_up, gate_up, k_k, k_a, r_k, gn_w, gn_b, pool_w, pool_scale, w_out,
             ffn_gate, ffn_up, ffn_down, final_norm):
    B, T, D = x.shape
    L = ada_w.shape[0]
    RW = k_k.shape[1]
    npair = RW // LANES
    ones_bd = jnp.asarray(np.kron(np.eye(LANES // HEAD), np.ones((HEAD, HEAD))), BF16)

    cond = jnp.zeros((8, D), F32).at[:B].set(c).at[B].set(c_ctx)
    mods = _ada_call(cond, ada_w, ada_b, cfg["tn_ada"])
    mods = mods.reshape(L, 8, 6, D)

    ctx_h = ctx
    for l in range(L):
        last = l == L - 1
        p = _layer_params(l, w_in, conv_rkv, decay_bias, decay_up, iclr_bias, iclr_up, gate_up, k_k, k_a,
                          r_k, gn_w, gn_b, pool_w, pool_scale, w_out, ffn_gate, ffn_up, ffn_down)
        p["ones_bd"] = ones_bd
        m_lat = [mods[l, :B, s].reshape(B, 1, D) for s in range(6)]
        m_ctx = [jnp.broadcast_to(mods[l, B, s].reshape(1, 1, D), (B, 1, D)) for s in range(6)]
        g_mix = norm_mix[l].reshape(1, D)
        g_ffn = norm_ffn[l].reshape(1, D)
        zero = jnp.zeros((B, npair, LANES, LANES), F32)

        def mixer(h, m, s0f, s0b, need_out):
            u = _inproj_call(h, g_mix, m[0], m[1], p["w_in"], cfg["tm_in"], cfg["tn_in"])
            r, k, v, kk, a0, a1, lw0, lw1, g, bonus = _prep_call(u, p, cfg["tt_prep"])
            yf, yb, sf, sb = _scan_call((r, k, v, kk), a0, a1, lw0, lw1, s0f, s0b, p["k_a"],
                                        cfg["scan_c"], cfg["scan_g"], cfg["scan_tb"])
            if not need_out:
                return None, None, sf, sb
            y = _post_call(yf, yb, bonus, g, p["gn_w"], p["gn_b"], ones_bd, cfg["tt_post"])
            return u, y, sf, sb

        def residuals(h, m, y, pooled):
            h = _outproj_call(h, y, pooled, m[2], p["w_out"], cfg["tm_out"])
            return _ffn_call(h, g_ffn, m[3], m[4], m[5], p["wg"], p["wu"], p["wd"],
                             cfg["tm_ffn"], cfg["tf_ffn"])

        u_ctx, y_ctx, s_fwd, s_bwd = mixer(ctx_h, m_ctx, zero, zero, not last)
        u_lat, y_lat, _, _ = mixer(x, m_lat, s_fwd, s_bwd, True)
        p_lat = _pool2d_call(u_lat, p["pool_w"], p["pool_scale"], cfg["tp_pool"])
        x = residuals(x, m_lat, y_lat, p_lat)
        if not last:
            p_ctx = _pool1d_call(u_ctx, p["pool_w"], p["pool_scale"])
            ctx_h = residuals(ctx_h, m_ctx, y_ctx, p_ctx)
    return _rms_call(x, final_norm.reshape(1, D), cfg["tm_rms"])


def kernel(x, c, ctx, c_ctx, ada_w, ada_b, norm_mix, norm_ffn, w_in, conv_rkv, decay_bias, decay_up,
           iclr_bias, iclr_up, gate_up, k_k, k_a, r_k, gn_w, gn_b, pool_w, pool_scale, w_out,
           ffn_gate, ffn_up, ffn_down, final_norm):
    return _forward(_CFG, x, c, ctx, c_ctx, ada_w, ada_b, norm_mix, norm_ffn, w_in, conv_rkv, decay_bias,
                    decay_up, iclr_bias, iclr_up, gate_up, k_k, k_a, r_k, gn_w, gn_b, pool_w, pool_scale,
                    w_out, ffn_gate, ffn_up, ffn_down, final_norm)
```

```python
import functools

import numpy as np
import jax
import jax.numpy as jnp
from jax import lax
from jax.experimental import pallas as pl
from jax.experimental.pallas import tpu as pltpu

F32 = jnp.float32
BF16 = jnp.bfloat16

HEAD = 64
LANES = 128
GRID_COLS = 64
POOL_WINS = (2, 4, 8, 16)
POOL_HALO_ROWS = 8
RMS_EPS = 1e-6
GN_EPS = 64e-5
DECAY_LORA = 96
ICLR_LORA = 96
GATE_LORA = 256
LORA_PAD = 128
VMEM_LIMIT = 56 * 1024 * 1024


def _cparams(sem):
    return pltpu.CompilerParams(dimension_semantics=sem, vmem_limit_bytes=VMEM_LIMIT)


def _dot(a, b):
    return jnp.dot(a, b, preferred_element_type=F32)


def _dot_nt(a, b):
    return lax.dot_general(a, b, (((1,), (1,)), ((), ())), preferred_element_type=F32)


def _dot_tn(a, b):
    return lax.dot_general(a, b, (((0,), (0,)), ((), ())), preferred_element_type=F32)


def _split(x):
    hi = x.astype(BF16)
    lo = (x - hi.astype(F32)).astype(BF16)
    return hi, lo


def _head_sum(x, ones_bd):
    hi, lo = _split(x)
    parts = []
    for c in range(x.shape[1] // LANES):
        sl = slice(c * LANES, (c + 1) * LANES)
        parts.append(_dot(hi[:, sl], ones_bd) + _dot(lo[:, sl], ones_bd))
    return parts[0] if len(parts) == 1 else jnp.concatenate(parts, axis=1)


def _sigmoid(x):
    return 1.0 / (1.0 + jnp.exp(-x))


def _silu(x):
    return x * _sigmoid(x)


def _ada_kernel(c_ref, w_ref, b_ref, o_ref):
    c = c_ref[...]
    o_ref[0] = _dot(_silu(c).astype(BF16), w_ref[0].astype(BF16)) + b_ref[0]


def _ada_call(cond, ada_w, ada_b, tn):
    L, D, N = ada_w.shape
    R = cond.shape[0]
    return pl.pallas_call(
        _ada_kernel,
        grid=(L, N // tn),
        in_specs=[
            pl.BlockSpec((R, D), lambda l, j: (0, 0)),
            pl.BlockSpec((1, D, tn), lambda l, j: (l, 0, j)),
            pl.BlockSpec((1, 1, tn), lambda l, j: (l, 0, j)),
        ],
        out_specs=pl.BlockSpec((1, R, tn), lambda l, j: (l, 0, j)),
        out_shape=jax.ShapeDtypeStruct((L, R, N), F32),
        compiler_params=_cparams(("parallel", "parallel")),
        name="ada_modulation",
    )(cond, ada_w, ada_b.reshape(L, 1, N))


def _modnorm(x, g, shift, scale):
    ms = jnp.mean(x * x, axis=-1, keepdims=True)
    return (x * lax.rsqrt(ms + RMS_EPS) * g) * (1.0 + scale) + shift


def _inproj_kernel(x_ref, g_ref, sh_ref, sc_ref, w_ref, o_ref, xn_ref):
    @pl.when(pl.program_id(2) == 0)
    def _():
        xn_ref[...] = _modnorm(x_ref[0], g_ref[...], sh_ref[0], sc_ref[0]).astype(BF16)

    o_ref[0] = _dot(xn_ref[...], w_ref[...])


def _inproj_call(x, g, shift, scale, w, tm, tn):
    B, T, D = x.shape
    N = w.shape[1]
    tm = min(tm, T)
    return pl.pallas_call(
        _inproj_kernel,
        grid=(B, T // tm, N // tn),
        in_specs=[
            pl.BlockSpec((1, tm, D), lambda b, i, j: (b, i, 0)),
            pl.BlockSpec((1, D), lambda b, i, j: (0, 0)),
            pl.BlockSpec((1, 1, D), lambda b, i, j: (b, 0, 0)),
            pl.BlockSpec((1, 1, D), lambda b, i, j: (b, 0, 0)),
            pl.BlockSpec((D, tn), lambda b, i, j: (0, j)),
        ],
        out_specs=pl.BlockSpec((1, tm, tn), lambda b, i, j: (b, i, j)),
        out_shape=jax.ShapeDtypeStruct((B, T, N), F32),
        scratch_shapes=[pltpu.VMEM((tm, D), BF16)],
        compiler_params=_cparams(("parallel", "parallel", "arbitrary")),
        name="norm_inproj",
    )(x, g, shift, scale, w)


def _conv3(x, prev8, next8, w, has_prev, has_next):
    tt = x.shape[0]
    row = lax.broadcasted_iota(jnp.int32, x.shape, 0)
    up = jnp.where(row == 0, prev8[7:8, :] * has_prev, pltpu.roll(x, 1, 0))
    dn = jnp.where(row == tt - 1, next8[0:1, :] * has_next, pltpu.roll(x, tt - 1, 0))
    return up * w[0:1, :] + x * w[1:2, :] + dn * w[2:3, :]


def _prep_kernel(ur, uk, uv, pr, pk, pv, nr, nk, nv, lora, cw, dbias, dup, ibias, iup, gup,
                 kkw, ka, rk, ones_ref,
                 r_o, k_o, v_o, kk_o, a0_o, a1_o, lw0_o, lw1_o, g_o, bonus_o):
    i = pl.program_id(1)
    has_prev = (i > 0).astype(F32)
    has_next = (i < pl.num_programs(1) - 1).astype(F32)
    W = r_o.shape[2]
    ones_bd = ones_ref[...]
    r = _conv3(ur[0], pr[0], nr[0], cw[:, 0:W], has_prev, has_next)
    k = _conv3(uk[0], pk[0], nk[0], cw[:, W:2 * W], has_prev, has_next)
    v = _conv3(uv[0], pv[0], nv[0], cw[:, 2 * W:3 * W], has_prev, has_next)
    r_o[0] = r
    k_o[0] = k
    v_o[0] = v
    kk = k * kkw[...]
    kk = kk * lax.rsqrt(jnp.maximum(_head_sum(kk * kk, ones_bd), 1e-24))
    kk_o[0] = kk
    lo = lora[0]
    w_lo = jnp.tanh(lo[:, 0:LORA_PAD]).astype(BF16)
    a_lo = lo[:, LORA_PAD:2 * LORA_PAD].astype(BF16)
    g_lo = _sigmoid(lo[:, 2 * LORA_PAD:2 * LORA_PAD + GATE_LORA]).astype(BF16)
    a_sum = None
    for d, (a_o, lw_o) in enumerate(((a0_o, lw0_o), (a1_o, lw1_o))):
        z = dbias[d:d + 1, :] + _dot(w_lo, dup[d])
        w_log = jnp.minimum(z, 0.0) - jnp.log(1.0 + jnp.exp(-jnp.abs(z))) - 0.5
        lw_o[0] = -jnp.exp(w_log)
        a = _sigmoid(ibias[d:d + 1, :] + _dot(a_lo, iup[d]))
        a_o[0] = a
        a_sum = a if a_sum is None else a_sum + a
    g_o[0] = _dot(g_lo, gup[...])
    kpair = k * (2.0 + (a_sum - 2.0) * ka[...])
    bonus_o[0] = _head_sum(r * kpair * rk[...], ones_bd) * v


def _prep_call(u, p, tt):
    B, T, _ = u.shape
    W = p["k_k"].shape[1]
    tt = min(tt, T)
    nb8 = T // 8
    cur = lambda c: pl.BlockSpec((1, tt, W), lambda b, i: (b, i, c))
    prv = lambda c: pl.BlockSpec((1, 8, W), lambda b, i: (b, jnp.maximum(i * (tt // 8) - 1, 0), c))
    nxt = lambda c: pl.BlockSpec((1, 8, W), lambda b, i: (b, jnp.minimum((i + 1) * (tt // 8), nb8 - 1), c))
    full = lambda a: pl.BlockSpec(a.shape, lambda b, i: (0,) * a.ndim)
    lora_blk = (u.shape[2] - 2 * LORA_PAD - GATE_LORA) // (2 * LORA_PAD + GATE_LORA)
    consts = [p["conv"], p["dbias"], p["dup"], p["ibias"], p["iup"], p["gup"],
              p["k_k"], p["k_a"], p["r_k"], p["ones_bd"]]
    out_blk = pl.BlockSpec((1, tt, W), lambda b, i: (b, i, 0))
    return pl.pallas_call(
        _prep_kernel,
        grid=(B, T // tt),
        in_specs=[cur(0), cur(1), cur(2), prv(0), prv(1), prv(2), nxt(0), nxt(1), nxt(2),
                  pl.BlockSpec((1, tt, 2 * LORA_PAD + GATE_LORA), lambda b, i: (b, i, lora_blk))]
                 + [full(a) for a in consts],
        out_specs=[out_blk] * 10,
        out_shape=[jax.ShapeDtypeStruct((B, T, W), F32)] * 10,
        compiler_params=_cparams(("parallel", "parallel")),
        name="rwkv_prep",
    )(u, u, u, u, u, u, u, u, u, u, *consts)


def _scan_masks(C):
    P = 2 * C
    p = np.arange(P)[:, None]
    q = np.arange(P)[None, :]
    same = (p // C) == (q // C)
    t, j = p % C, q % C
    fwd = [same & (t > j), same & (t >= j)]
    rev = [same & (t < j), same & (t <= j)]
    s = 1
    while s < C:
        blk = (p // (2 * s)) == (q // (2 * s))
        fwd.append(blk & ((p % (2 * s)) >= s) & ((q % (2 * s)) < s))
        rev.append(blk & ((p % (2 * s)) < s) & ((q % (2 * s)) >= s))
        s *= 2
    return np.stack(fwd + rev).astype(np.float32)


def _chunk_group(chains, masks, nlev, m0, m1, eye):
    n = len(chains)
    C = chains[0][0].shape[0]
    P = 2 * C
    stack = lambda x: jnp.concatenate([x * m0, x * m1], axis=0)
    AR, Bmb, Kmb, Vsb = [], [], [], []
    for (At, Rt, Bt, Kt, v, _, _, _) in chains:
        AR.append(jnp.concatenate([stack(At), stack(Rt)], axis=0).astype(BF16))
        Bmb.append(stack(Bt).astype(BF16))
        Kmb.append(stack(Kt).astype(BF16))
        Vsb.append(stack(v).astype(BF16))
    scb = [_dot_nt(AR[i], Bmb[i]) for i in range(n)]
    sck = [_dot_nt(AR[i], Kmb[i]) for i in range(n)]
    x1 = [_dot_nt(AR[i], chains[i][6].astype(BF16)) for i in range(n)]
    Nab = [scb[i][:P] * masks[chains[i][7]] for i in range(n)]
    T = [eye + Nab[i] * masks[chains[i][7] + 2] for i in range(n)]
    for lv in range(1, nlev):
        Tb = [T[i].astype(BF16) for i in range(n)]
        TN = [_dot(Tb[i], (Nab[i] * masks[chains[i][7] + 2 + lv]).astype(BF16)).astype(BF16) for i in range(n)]
        T = [T[i] + _dot(TN[i], Tb[i]) for i in range(n)]
    NV = [_dot((sck[i][:P] * masks[chains[i][7]]).astype(BF16), Vsb[i]) for i in range(n)]
    Us = [_dot(T[i].astype(BF16), (x1[i][:P] + NV[i]).astype(BF16)) for i in range(n)]
    UV = [jnp.concatenate([Us[i].astype(BF16), Vsb[i]], axis=0) for i in range(n)]
    out = []
    for i in range(n):
        incl = masks[chains[i][7] + 1]
        Pr = jnp.concatenate([scb[i][P:] * incl, sck[i][P:] * incl], axis=1).astype(BF16)
        Ys = x1[i][P:] + _dot(Pr, UV[i])
        Zn = chains[i][5] * (chains[i][6] + _dot_tn(UV[i], jnp.concatenate([Bmb[i], Kmb[i]], axis=0)))
        out.append((Ys[:C] + Ys[C:], Zn))
    return out


def _scan_kernel(rf, kf, vf, kkf, af, lwf, rb, kb, vb, kkb, ab, lwb, s0f, s0b, ka_ref, tri_ref, m_ref,
                 yf_o, yb_o, sf_o, sb_o, *, C, G, nchunk):
    @pl.when(pl.program_id(2) == 0)
    def _():
        sf_o[...] = s0f[...]
        sb_o[...] = s0b[...]

    nlev = int(np.log2(C))
    nmask = 2 + nlev
    lane = lax.broadcasted_iota(jnp.int32, (1, LANES), 1)
    m0 = (lane < HEAD).astype(F32)
    m1 = 1.0 - m0
    ri = lax.broadcasted_iota(jnp.int32, (2 * C, 2 * C), 0)
    ci = lax.broadcasted_iota(jnp.int32, (2 * C, 2 * C), 1)
    eye = (ri == ci).astype(F32)
    ka = ka_ref[...]
    dirs = ((rf, kf, vf, kkf, af, lwf, yf_o, sf_o, False), (rb, kb, vb, kkb, ab, lwb, yb_o, sb_o, True))
    for cidx in range(nchunk):
        chains, dests = [], []
        for (r_r, k_r, v_r, kk_r, a_r, lw_r, y_o, s_o, rev) in dirs:
            c = (nchunk - 1 - cidx) if rev else cidx
            rows = slice(c * C, (c + 1) * C)
            r, k, v, kk = r_r[0, rows, :], k_r[0, rows, :], v_r[0, rows, :], kk_r[0, rows, :]
            a, lw = a_r[0, rows, :], lw_r[0, rows, :]
            hi, lo = _split(lw)
            tri = tri_ref[1 if rev else 0]
            cum = _dot(tri, hi) + _dot(tri, lo)
            ea = jnp.exp(cum - lw)
            er = jnp.exp(cum)
            einv = jnp.exp(-cum)
            At = -(kk * ea)
            Rt = r * er
            Bt = (kk * a) * einv
            Kt = (k * (1.0 + (a - 1.0) * ka)) * einv
            gam = er[0:1, :] if rev else er[C - 1:C, :]
            for g in range(G):
                sl = slice(g * LANES, (g + 1) * LANES)
                chains.append((At[:, sl], Rt[:, sl], Bt[:, sl], Kt[:, sl], v[:, sl], gam[:, sl],
                               s_o[0, g], nmask if rev else 0))
            dests.append((y_o, s_o, rows))
        res = _chunk_group(chains, m_ref, nlev, m0, m1, eye)
        for d, (y_o, s_o, rows) in enumerate(dests):
            part = res[d * G:(d + 1) * G]
            for g in range(G):
                s_o[0, g] = part[g][1]
            y_o[0, rows, :] = part[0][0] if G == 1 else jnp.concatenate([q[0] for q in part], axis=1)


def _scan_call(pre, a0, a1, lw0, lw1, s0f, s0b, ka, C, G, TB):
    r, k, v, kk = pre
    B, T, W = r.shape
    TB = min(TB, T)
    NT = T // TB
    NG = W // (G * LANES)
    GW = G * LANES
    fwd = pl.BlockSpec((1, TB, GW), lambda b, g, i: (b, i, g))
    bwd = pl.BlockSpec((1, TB, GW), lambda b, g, i: (b, NT - 1 - i, g))
    st = pl.BlockSpec((1, G, LANES, LANES), lambda b, g, i: (b, g, 0, 0))
    masks = jnp.asarray(_scan_masks(C))
    tri = jnp.asarray(np.stack([np.tril(np.ones((C, C), np.float32)),
                                np.triu(np.ones((C, C), np.float32))])).astype(BF16)
    kern = functools.partial(_scan_kernel, C=C, G=G, nchunk=TB // C)
    return pl.pallas_call(
        kern,
        grid=(B, NG, NT),
        in_specs=[fwd] * 6 + [bwd] * 6 + [st, st,
                  pl.BlockSpec((1, GW), lambda b, g, i: (0, g)),
                  pl.BlockSpec(tri.shape, lambda b, g, i: (0, 0, 0)),
                  pl.BlockSpec(masks.shape, lambda b, g, i: (0, 0, 0))],
        out_specs=[fwd, bwd, st, st],
        out_shape=[jax.ShapeDtypeStruct((B, T, W), F32)] * 2
                  + [jax.ShapeDtypeStruct(s0f.shape, F32)] * 2,
        compiler_params=_cparams(("parallel", "parallel", "arbitrary")),
        name="wkv7_scan",
    )(r, k, v, kk, a0, lw0, r, k, v, kk, a1, lw1, s0f, s0b, ka, tri, masks)


def _post_kernel(yf, yb, bonus, g, gnw, gnb, ones_ref, o_ref):
    ones_bd = ones_ref[...]
    y = yf[0] + yb[0]
    mu = _head_sum(y, ones_bd) * (1.0 / HEAD)
    yc = y - mu
    var = _head_sum(yc * yc, ones_bd) * (1.0 / HEAD)
    yn = yc * lax.rsqrt(var + GN_EPS) * gnw[...] + gnb[...]
    o_ref[0] = (yn + bonus[0]) * g[0]


def _post_call(yf, yb, bonus, g, gnw, gnb, ones_bd, tt):
    B, T, W = yf.shape
    tt = min(tt, T)
    blk = pl.BlockSpec((1, tt, W), lambda b, i: (b, i, 0))
    vec = pl.BlockSpec((1, W), lambda b, i: (0, 0))
    return pl.pallas_call(
        _post_kernel,
        grid=(B, T // tt),
        in_specs=[blk, blk, blk, blk, vec, vec, pl.BlockSpec(ones_bd.shape, lambda b, i: (0, 0))],
        out_specs=blk,
        out_shape=jax.ShapeDtypeStruct((B, T, W), F32),
        compiler_params=_cparams(("parallel", "parallel")),
        name="rwkv_post",
    )(yf, yb, bonus, g, gnw, gnb, ones_bd)


def _window_count(pos, half, n):
    return (jnp.minimum(pos + half, n) - jnp.maximum(pos - half, 0)).astype(F32)


def _pool2d_kernel(cur, prv, nxt, pw_ref, ps_ref, o_ref, col_ref, *, tp, halo, nrows):
    i = pl.program_id(1)
    has_prev = (i > 0).astype(F32)
    has_next = (i < pl.num_programs(1) - 1).astype(F32)
    PG = pw_ref.shape[1]
    shift = GRID_COLS.bit_length() - 1
    tok = lax.broadcasted_iota(jnp.int32, (tp, PG), 0)
    col = jnp.bitwise_and(tok, GRID_COLS - 1)
    row = i * (tp // GRID_COLS) + jnp.right_shift(tok, shift)
    p = lax.broadcasted_iota(jnp.int32, (LANES, LANES), 0)
    q = lax.broadcasted_iota(jnp.int32, (LANES, LANES), 1)
    same_row = jnp.right_shift(p, shift) == jnp.right_shift(q, shift)
    dcol = jnp.bitwise_and(p, GRID_COLS - 1) - jnp.bitwise_and(q, GRID_COLS - 1)
    for gi, win in enumerate(POOL_WINS):
        half = win // 2
        cs = slice(gi * PG, (gi + 1) * PG)
        band = (same_row & (dcol <= half) & (dcol > -half)).astype(BF16)

        def colsum(x):
            hi, lo = _split(x)
            return _dot(band, hi) + _dot(band, lo)

        for c in range(halo // LANES):
            rs = slice(c * LANES, (c + 1) * LANES)
            col_ref[rs, :] = colsum(prv[0, rs, cs] * has_prev)
            rs2 = slice(halo + tp + c * LANES, halo + tp + (c + 1) * LANES)
            col_ref[rs2, :] = colsum(nxt[0, rs, cs] * has_next)
        for c in range(tp // LANES):
            rs = slice(c * LANES, (c + 1) * LANES)
            col_ref[halo + c * LANES:halo + (c + 1) * LANES, :] = colsum(cur[0, rs, cs])
        acc = None
        for dr in range(-half, half):
            part = col_ref[halo + dr * GRID_COLS:halo + dr * GRID_COLS + tp, :]
            acc = part if acc is None else acc + part
        cnt = _window_count(col, half, GRID_COLS) * _window_count(row, half, nrows)
        diff = acc * (1.0 / cnt) - cur[0, :, cs]
        o_ref[0, :, cs] = _dot(diff.astype(BF16), pw_ref[gi]) * ps_ref[:, cs]


def _pool2d_call(u, pool_w, pool_scale, tp):
    B, T, _ = u.shape
    PW = pool_scale.shape[1]
    halo = POOL_HALO_ROWS * GRID_COLS
    tp = min(tp, T)
    cblk = (u.shape[2] - 2 * LORA_PAD - GATE_LORA - PW) // PW
    nh = T // halo
    kern = functools.partial(_pool2d_kernel, tp=tp, halo=halo, nrows=T // GRID_COLS)
    return pl.pallas_call(
        kern,
        grid=(B, T // tp),
        in_specs=[
            pl.BlockSpec((1, tp, PW), lambda b, i: (b, i, cblk)),
            pl.BlockSpec((1, halo, PW), lambda b, i: (b, jnp.maximum(i * (tp // halo) - 1, 0), cblk)),
            pl.BlockSpec((1, halo, PW), lambda b, i: (b, jnp.minimum((i + 1) * (tp // halo), nh - 1), cblk)),
            pl.BlockSpec(pool_w.shape, lambda b, i: (0, 0, 0)),
            pl.BlockSpec((1, PW), lambda b, i: (0, 0)),
        ],
        out_specs=pl.BlockSpec((1, tp, PW), lambda b, i: (b, i, 0)),
        out_shape=jax.ShapeDtypeStruct((B, T, PW), F32),
        scratch_shapes=[pltpu.VMEM((tp + 2 * halo, pool_w.shape[1]), F32)],
        compiler_params=_cparams(("parallel", "parallel")),
        name="pool2d",
    )(u, u, u, pool_w, pool_scale)


def _pool1d_kernel(u_ref, pw_ref, ps_ref, o_ref):
    T = u_ref.shape[1]
    PG = pw_ref.shape[1]
    p = lax.broadcasted_iota(jnp.int32, (T, T), 0)
    q = lax.broadcasted_iota(jnp.int32, (T, T), 1)
    pos = lax.broadcasted_iota(jnp.int32, (T, PG), 0)
    for gi, win in enumerate(POOL_WINS):
        half = win // 2
        cs = slice(gi * PG, (gi + 1) * PG)
        band = ((p - q <= half) & (p - q > -half)).astype(BF16)
        x = u_ref[0, :, cs]
        hi, lo = _split(x)
        m = (_dot(band, hi) + _dot(band, lo)) * (1.0 / _window_count(pos, half, T))
        o_ref[0, :, cs] = _dot((m - x).astype(BF16), pw_ref[gi]) * ps_ref[:, cs]


def _pool1d_call(u, pool_w, pool_scale):
    B, T, _ = u.shape
    PW = pool_scale.shape[1]
    cblk = (u.shape[2] - 2 * LORA_PAD - GATE_LORA - PW) // PW
    return pl.pallas_call(
        _pool1d_kernel,
        grid=(B,),
        in_specs=[
            pl.BlockSpec((1, T, PW), lambda b: (b, 0, cblk)),
            pl.BlockSpec(pool_w.shape, lambda b: (0, 0, 0)),
            pl.BlockSpec((1, PW), lambda b: (0, 0)),
        ],
        out_specs=pl.BlockSpec((1, T, PW), lambda b: (b, 0, 0)),
        out_shape=jax.ShapeDtypeStruct((B, T, PW), F32),
        compiler_params=_cparams(("parallel",)),
        name="pool1d",
    )(u, pool_w, pool_scale)


def _outproj_kernel(x_ref, y_ref, p_ref, gt_ref, w_ref, o_ref):
    W = y_ref.shape[2]
    acc = _dot(y_ref[0].astype(BF16), w_ref[0:W, :]) + _dot(p_ref[0].astype(BF16), w_ref[W:, :])
    o_ref[0] = x_ref[0] + gt_ref[0] * acc


def _outproj_call(x, y, p, gate, w, tm):
    B, T, D = x.shape
    W, PW = y.shape[2], p.shape[2]
    tm = min(tm, T)
    return pl.pallas_call(
        _outproj_kernel,
        grid=(B, T // tm),
        in_specs=[
            pl.BlockSpec((1, tm, D), lambda b, i: (b, i, 0)),
            pl.BlockSpec((1, tm, W), lambda b, i: (b, i, 0)),
            pl.BlockSpec((1, tm, PW), lambda b, i: (b, i, 0)),
            pl.BlockSpec((1, 1, D), lambda b, i: (b, 0, 0)),
            pl.BlockSpec(w.shape, lambda b, i: (0, 0)),
        ],
        out_specs=pl.BlockSpec((1, tm, D), lambda b, i: (b, i, 0)),
        out_shape=jax.ShapeDtypeStruct((B, T, D), F32),
        compiler_params=_cparams(("parallel", "parallel")),
        name="outproj_residual",
    )(x, y, p, gate, w)


def _ffn_kernel(x_ref, g_ref, sh_ref, sc_ref, gt_ref, wg_ref, wu_ref, wd_ref, o_ref, xn_ref, acc_ref):
    j = pl.program_id(2)

    @pl.when(j == 0)
    def _():
        xn_ref[...] = _modnorm(x_ref[0], g_ref[...], sh_ref[0], sc_ref[0]).astype(BF16)
        acc_ref[...] = jnp.zeros_like(acc_ref)

    xn = xn_ref[...]
    h = _silu(_dot(xn, wg_ref[...])) * _dot(xn, wu_ref[...])
    acc_ref[...] += _dot(h.astype(BF16), wd_ref[...])

    @pl.when(j == pl.num_programs(2) - 1)
    def _():
        o_ref[0] = x_ref[0] + gt_ref[0] * acc_ref[...]


def _ffn_call(x, g, shift, scale, gate, wg, wu, wd, tm, tf):
    B, T, D = x.shape
    F = wg.shape[1]
    tm = min(tm, T)
    mod = pl.BlockSpec((1, 1, D), lambda b, i, j: (b, 0, 0))
    return pl.pallas_call(
        _ffn_kernel,
        grid=(B, T // tm, F // tf),
        in_specs=[
            pl.BlockSpec((1, tm, D), lambda b, i, j: (b, i, 0)),
            pl.BlockSpec((1, D), lambda b, i, j: (0, 0)),
            mod, mod, mod,
            pl.BlockSpec((D, tf), lambda b, i, j: (0, j)),
            pl.BlockSpec((D, tf), lambda b, i, j: (0, j)),
            pl.BlockSpec((tf, D), lambda b, i, j: (j, 0)),
        ],
        out_specs=pl.BlockSpec((1, tm, D), lambda b, i, j: (b, i, 0)),
        out_shape=jax.ShapeDtypeStruct((B, T, D), F32),
        scratch_shapes=[pltpu.VMEM((tm, D), BF16), pltpu.VMEM((tm, D), F32)],
        compiler_params=_cparams(("parallel", "parallel", "arbitrary")),
        name="norm_swiglu_residual",
    )(x, g, shift, scale, gate, wg, wu, wd)


def _rms_kernel(x_ref, g_ref, o_ref):
    x = x_ref[0]
    ms = jnp.mean(x * x, axis=-1, keepdims=True)
    o_ref[0] = x * lax.rsqrt(ms + RMS_EPS) * g_ref[...]


def _rms_call(x, g, tm):
    B, T, D = x.shape
    tm = min(tm, T)
    blk = pl.BlockSpec((1, tm, D), lambda b, i: (b, i, 0))
    return pl.pallas_call(
        _rms_kernel,
        grid=(B, T // tm),
        in_specs=[blk, pl.BlockSpec((1, D), lambda b, i: (0, 0))],
        out_specs=blk,
        out_shape=jax.ShapeDtypeStruct((B, T, D), F32),
        compiler_params=_cparams(("parallel", "parallel")),
        name="final_rmsnorm",
    )(x, g)


def _pad_rows(w, n):
    return jnp.pad(w, ((0, 0),) * (w.ndim - 2) + ((0, n - w.shape[-2]), (0, 0)))


def _layer_params(l, w_in, conv_rkv, decay_bias, decay_up, iclr_bias, iclr_up, gate_up, k_k, k_a, r_k,
                  gn_w, gn_b, pool_w, pool_scale, w_out, ffn_gate, ffn_up, ffn_down):
    RW = k_k.shape[1]
    wl = w_in[l]
    o_w = 3 * RW
    o_a = o_w + DECAY_LORA
    o_g = o_a + ICLR_LORA
    o_p = o_g + GATE_LORA
    padc = lambda w: jnp.pad(w, ((0, 0), (0, LORA_PAD - w.shape[1])))
    w_in_r = jnp.concatenate(
        [wl[:, :o_w], wl[:, o_p:], padc(wl[:, o_w:o_a]), padc(wl[:, o_a:o_g]), wl[:, o_g:o_p]], axis=1)
    row = lambda a: a[l].reshape(1, -1)
    return {
        "w_in": w_in_r.astype(BF16),
        "conv": conv_rkv[l],
        "dbias": decay_bias[l], "dup": _pad_rows(decay_up[l], LORA_PAD).astype(BF16),
        "ibias": iclr_bias[l], "iup": _pad_rows(iclr_up[l], LORA_PAD).astype(BF16),
        "gup": gate_up[l].astype(BF16),
        "k_k": row(k_k), "k_a": row(k_a), "r_k": row(r_k), "gn_w": row(gn_w), "gn_b": row(gn_b),
        "pool_w": pool_w[l].astype(BF16), "pool_scale": row(pool_scale),
        "w_out": w_out[l].astype(BF16),
        "wg": ffn_gate[l].astype(BF16), "wu": ffn_up[l].astype(BF16), "wd": ffn_down[l].astype(BF16),
    }


_CFG = dict(tm_in=512, tn_in=1536, tt_prep=256, scan_c=64, scan_g=8, scan_tb=64, tt_post=512,
            tp_pool=2048, tm_out=512, tm_ffn=512, tf_ffn=512, tm_rms=512, tn_ada=1024)


def _forward(cfg, x, c, ctx, c_ctx, ada_w, ada_b, norm_mix, norm_ffn, w_in, conv_rkv, decay_bias, decay_up,
             iclr_bias, iclr_up, gate_up, k_k, k_a, r_k, gn_w, gn_b, pool_w, pool_scale, w_out,
             ffn_gate, ffn_up, ffn_down, final_norm):
    B, T, D = x.shape
    L = ada_w.shape[0]
    RW = k_k.shape[1]
    npair = RW // LANES
    ones_bd = jnp.asarray(np.kron(np.eye(LANES // HEAD), np.ones((HEAD, HEAD))), BF16)

    cond = jnp.zeros((8, D), F32).at[:B].set(c).at[B].set(c_ctx)
    mods = _ada_call(cond, ada_w, ada_b, cfg["tn_ada"])
    mods = mods.reshape(L, 8, 6, D)

    ctx_h = ctx
    for l in range(L):
        last = l == L - 1
        p = _layer_params(l, w_in, conv_rkv, decay_bias, decay_up, iclr_bias, iclr_up, gate_up, k_k, k_a,
                          r_k, gn_w, gn_b, pool_w, pool_scale, w_out, ffn_gate, ffn_up, ffn_down)
        p["ones_bd"] = ones_bd
        m_lat = [mods[l, :B, s].reshape(B, 1, D) for s in range(6)]
        m_ctx = [jnp.broadcast_to(mods[l, B, s].reshape(1, 1, D), (B, 1, D)) for s in range(6)]
        g_mix = norm_mix[l].reshape(1, D)
        g_ffn = norm_ffn[l].reshape(1, D)
        zero = jnp.zeros((B, npair, LANES, LANES), F32)

        def mixer(h, m, s0f, s0b, need_out):
            u = _inproj_call(h, g_mix, m[0], m[1], p["w_in"], cfg["tm_in"], cfg["tn_in"])
            r, k, v, kk, a0, a1, lw0, lw1, g, bonus = _prep_call(u, p, cfg["tt_prep"])
            yf, yb, sf, sb = _scan_call((r, k, v, kk), a0, a1, lw0, lw1, s0f, s0b, p["k_a"],
                                        cfg["scan_c"], cfg["scan_g"], cfg["scan_tb"])
            if not need_out:
                return None, None, sf, sb
            y = _post_call(yf, yb, bonus, g, p["gn_w"], p["gn_b"], ones_bd, cfg["tt_post"])
            return u, y, sf, sb

        def residuals(h, m, y, pooled):
            h = _outproj_call(h, y, pooled, m[2], p["w_out"], cfg["tm_out"])
            return _ffn_call(h, g_ffn, m[3], m[4], m[5], p["wg"], p["wu"], p["wd"],
                             cfg["tm_ffn"], cfg["tf_ffn"])

        u_ctx, y_ctx, s_fwd, s_bwd = mixer(ctx_h, m_ctx, zero, zero, not last)
        u_lat, y_lat, _, _ = mixer(x, m_lat, s_fwd, s_bwd, True)
        p_lat = _pool2d_call(u_lat, p["pool_w"], p["pool_scale"], cfg["tp_pool"])
        x = residuals(x, m_lat, y_lat, p_lat)
        if not last:
            p_ctx = _pool1d_call(u_ctx, p["pool_w"], p["pool_scale"])
            ctx_h = residuals(ctx_h, m_ctx, y_ctx, p_ctx)
    return _rms_call(x, final_norm.reshape(1, D), cfg["tm_rms"])


def kernel(x, c, ctx, c_ctx, ada_w, ada_b, norm_mix, norm_ffn, w_in, conv_rkv, decay_bias, decay_up,
           iclr_bias, iclr_up, gate_up, k_k, k_a, r_k, gn_w, gn_b, pool_w, pool_scale, w_out,
           ffn_gate, ffn_up, ffn_down, final_norm):
    return _forward(_CFG, x, c, ctx, c_ctx, ada_w, ada_b, norm_mix, norm_ffn, w_in, conv_rkv, decay_bias,
                    decay_up, iclr_bias, iclr_up, gate_up, k_k, k_a, r_k, gn_w, gn_b, pool_w, pool_scale,
                    w_out, ffn_gate, ffn_up, ffn_down, final_norm)
```

```python
import functools

import numpy as np
import jax
import jax.numpy as jnp
from jax import lax
from jax.experimental import pallas as pl
from jax.experimental.pallas import tpu as pltpu

F32 = jnp.float32
BF16 = jnp.bfloat16

HEAD = 64
LANES = 128
GRID_COLS = 64
POOL_WINS = (2, 4, 8, 16)
POOL_HALO_ROWS = 8
RMS_EPS = 1e-6
GN_EPS = 64e-5
DECAY_LORA = 96
ICLR_LORA = 96
GATE_LORA = 256
LORA_PAD = 128
VMEM_LIMIT = 56 * 1024 * 1024


def _cparams(sem):
    return pltpu.CompilerParams(dimension_semantics=sem, vmem_limit_bytes=VMEM_LIMIT)


def _dot(a, b):
    return jnp.dot(a, b, preferred_element_type=F32)


def _dot_nt(a, b):
    return lax.dot_general(a, b, (((1,), (1,)), ((), ())), preferred_element_type=F32)


def _dot_tn(a, b):
    return lax.dot_general(a, b, (((0,), (0,)), ((), ())), preferred_element_type=F32)


def _split(x):
    hi = x.astype(BF16)
    lo = (x - hi.astype(F32)).astype(BF16)
    return hi, lo


def _head_sum(x, ones_bd):
    hi, lo = _split(x)
    parts = []
    for c in range(x.shape[1] // LANES):
        sl = slice(c * LANES, (c + 1) * LANES)
        parts.append(_dot(hi[:, sl], ones_bd) + _dot(lo[:, sl], ones_bd))
    return parts[0] if len(parts) == 1 else jnp.concatenate(parts, axis=1)


def _sigmoid(x):
    return 1.0 / (1.0 + jnp.exp(-x))


def _silu(x):
    return x * _sigmoid(x)


def _ada_kernel(c_ref, w_ref, b_ref, o_ref):
    c = c_ref[...]
    o_ref[0] = _dot(_silu(c).astype(BF16), w_ref[0].astype(BF16)) + b_ref[0]


def _ada_call(cond, ada_w, ada_b, tn):
    L, D, N = ada_w.shape
    R = cond.shape[0]
    return pl.pallas_call(
        _ada_kernel,
        grid=(L, N // tn),
        in_specs=[
            pl.BlockSpec((R, D), lambda l, j: (0, 0)),
            pl.BlockSpec((1, D, tn), lambda l, j: (l, 0, j)),
            pl.BlockSpec((1, 1, tn), lambda l, j: (l, 0, j)),
        ],
        out_specs=pl.BlockSpec((1, R, tn), lambda l, j: (l, 0, j)),
        out_shape=jax.ShapeDtypeStruct((L, R, N), F32),
        compiler_params=_cparams(("parallel", "parallel")),
        name="ada_modulation",
    )(cond, ada_w, ada_b.reshape(L, 1, N))


def _modnorm(x, g, shift, scale):
    ms = jnp.mean(x * x, axis=-1, keepdims=True)
    return (x * lax.rsqrt(ms + RMS_EPS) * g) * (1.0 + scale) + shift


def _inproj_kernel(x_ref, g_ref, sh_ref, sc_ref, w_ref, o_ref):
    xn = _modnorm(x_ref[0], g_ref[...], sh_ref[0], sc_ref[0]).astype(BF16)
    o_ref[0] = _dot(xn, w_ref[...])


def _inproj_call(x, g, shift, scale, w, tm):
    B, T, D = x.shape
    N = w.shape[1]
    tm = min(tm, T)
    return pl.pallas_call(
        _inproj_kernel,
        grid=(B, T // tm),
        in_specs=[
            pl.BlockSpec((1, tm, D), lambda b, i: (b, i, 0)),
            pl.BlockSpec((1, D), lambda b, i: (0, 0)),
            pl.BlockSpec((1, 1, D), lambda b, i: (b, 0, 0)),
            pl.BlockSpec((1, 1, D), lambda b, i: (b, 0, 0)),
            pl.BlockSpec((D, N), lambda b, i: (0, 0), pipeline_mode=pl.Buffered(1)),
        ],
        out_specs=pl.BlockSpec((1, tm, N), lambda b, i: (b, i, 0)),
        out_shape=jax.ShapeDtypeStruct((B, T, N), F32),
        compiler_params=_cparams(("parallel", "parallel")),
        name="norm_inproj",
    )(x, g, shift, scale, w)


def _conv3(x, prev8, next8, w, has_prev, has_next):
    tt = x.shape[0]
    row = lax.broadcasted_iota(jnp.int32, x.shape, 0)
    up = jnp.where(row == 0, prev8[7:8, :] * has_prev, pltpu.roll(x, 1, 0))
    dn = jnp.where(row == tt - 1, next8[0:1, :] * has_next, pltpu.roll(x, tt - 1, 0))
    return up * w[0:1, :] + x * w[1:2, :] + dn * w[2:3, :]


def _prep_kernel(ur, uk, uv, pr, pk, pv, nr, nk, nv, lora, cw, dbias, dup, ibias, iup, gup,
                 kkw, ka, rk, ones_ref,
                 r_o, k_o, v_o, kk_o, a0_o, a1_o, lw0_o, lw1_o, g_o, bonus_o):
    i = pl.program_id(1)
    has_prev = (i > 0).astype(F32)
    has_next = (i < pl.num_programs(1) - 1).astype(F32)
    W = r_o.shape[2]
    ones_bd = ones_ref[...]
    r = _conv3(ur[0], pr[0], nr[0], cw[:, 0:W], has_prev, has_next)
    k = _conv3(uk[0], pk[0], nk[0], cw[:, W:2 * W], has_prev, has_next)
    v = _conv3(uv[0], pv[0], nv[0], cw[:, 2 * W:3 * W], has_prev, has_next)
    r_o[0] = r
    k_o[0] = k
    v_o[0] = v
    kk = k * kkw[...]
    kk = kk * lax.rsqrt(jnp.maximum(_head_sum(kk * kk, ones_bd), 1e-24))
    kk_o[0] = kk
    lo = lora[0]
    w_lo = jnp.tanh(lo[:, 0:LORA_PAD]).astype(BF16)
    a_lo = lo[:, LORA_PAD:2 * LORA_PAD].astype(BF16)
    g_lo = _sigmoid(lo[:, 2 * LORA_PAD:2 * LORA_PAD + GATE_LORA]).astype(BF16)
    a_sum = None
    for d, (a_o, lw_o) in enumerate(((a0_o, lw0_o), (a1_o, lw1_o))):
        z = dbias[d:d + 1, :] + _dot(w_lo, dup[d])
        w_log = jnp.minimum(z, 0.0) - jnp.log(1.0 + jnp.exp(-jnp.abs(z))) - 0.5
        lw_o[0] = -jnp.exp(w_log)
        a = _sigmoid(ibias[d:d + 1, :] + _dot(a_lo, iup[d]))
        a_o[0] = a
        a_sum = a if a_sum is None else a_sum + a
    g_o[0] = _dot(g_lo, gup[...])
    kpair = k * (2.0 + (a_sum - 2.0) * ka[...])
    bonus_o[0] = _head_sum(r * kpair * rk[...], ones_bd) * v


def _prep_call(u, p, tt):
    B, T, _ = u.shape
    W = p["k_k"].shape[1]
    tt = min(tt, T)
    nb8 = T // 8
    cur = lambda c: pl.BlockSpec((1, tt, W), lambda b, i: (b, i, c))
    prv = lambda c: pl.BlockSpec((1, 8, W), lambda b, i: (b, jnp.maximum(i * (tt // 8) - 1, 0), c))
    nxt = lambda c: pl.BlockSpec((1, 8, W), lambda b, i: (b, jnp.minimum((i + 1) * (tt // 8), nb8 - 1), c))
    full = lambda a: pl.BlockSpec(a.shape, lambda b, i: (0,) * a.ndim)
    lora_blk = (u.shape[2] - 2 * LORA_PAD - GATE_LORA) // (2 * LORA_PAD + GATE_LORA)
    consts = [p["conv"], p["dbias"], p["dup"], p["ibias"], p["iup"], p["gup"],
              p["k_k"], p["k_a"], p["r_k"], p["ones_bd"]]
    out_blk = pl.BlockSpec((1, tt, W), lambda b, i: (b, i, 0))
    return pl.pallas_call(
        _prep_kernel,
        grid=(B, T // tt),
        in_specs=[cur(0), cur(1), cur(2), prv(0), prv(1), prv(2), nxt(0), nxt(1), nxt(2),
                  pl.BlockSpec((1, tt, 2 * LORA_PAD + GATE_LORA), lambda b, i: (b, i, lora_blk))]
                 + [full(a) for a in consts],
        out_specs=[out_blk] * 10,
        out_shape=[jax.ShapeDtypeStruct((B, T, W), F32)] * 10,
        compiler_params=_cparams(("parallel", "parallel")),
        name="rwkv_prep",
    )(u, u, u, u, u, u, u, u, u, u, *consts)


def _scan_masks(C):
    t = np.arange(C)[:, None]
    j = np.arange(2 * C)[None, :] % C
    fwd = [t > j, t >= j]
    rev = [t < j, t <= j]
    s = 1
    while s < C:
        blk = (t // (2 * s)) == (j // (2 * s))
        fwd.append(blk & ((t % (2 * s)) >= s) & ((j % (2 * s)) < s))
        rev.append(blk & ((t % (2 * s)) < s) & ((j % (2 * s)) >= s))
        s *= 2
    return np.stack(fwd + rev).astype(np.float32)


def _chunk_group(chains, masks, nlev, m0, m1, eye):
    n = len(chains)
    C = chains[0][0].shape[0]
    P = 2 * C
    stack = lambda x: jnp.concatenate([x * m0, x * m1], axis=0)
    AR, BK, Vs = [], [], []
    for (At, Rt, Bt, Kt, v, _, _, _) in chains:
        AR.append(jnp.concatenate([At, Rt], axis=0).astype(BF16))
        BK.append(jnp.concatenate([stack(Bt), stack(Kt)], axis=0).astype(BF16))
        Vs.append(stack(v).astype(BF16))
    sc = [_dot_nt(AR[i], BK[i]) for i in range(n)]
    x1 = [_dot_nt(AR[i], chains[i][6].astype(BF16)) for i in range(n)]
    Nab = [sc[i][:C, :P] * masks[chains[i][7]] for i in range(n)]
    T = [eye + Nab[i] * masks[chains[i][7] + 2] for i in range(n)]
    for lv in range(1, nlev):
        Tb = [T[i].astype(BF16) for i in range(n)]
        Nl = [stack(Nab[i] * masks[chains[i][7] + 2 + lv]).astype(BF16) for i in range(n)]
        TN = [_dot(Tb[i], Nl[i]).astype(BF16) for i in range(n)]
        T = [T[i] + _dot(TN[i], stack(T[i]).astype(BF16)) for i in range(n)]
    NV = [_dot((sc[i][:C, P:] * masks[chains[i][7]]).astype(BF16), Vs[i]) for i in range(n)]
    Us = [_dot(T[i].astype(BF16), stack(x1[i][:C] + NV[i]).astype(BF16)) for i in range(n)]
    UV = [jnp.concatenate([stack(Us[i]).astype(BF16), Vs[i]], axis=0) for i in range(n)]
    out = []
    for i in range(n):
        incl = masks[chains[i][7] + 1]
        Pr = jnp.concatenate([sc[i][C:, :P] * incl, sc[i][C:, P:] * incl], axis=1).astype(BF16)
        y = x1[i][C:] + _dot(Pr, UV[i])
        Zn = chains[i][5] * (chains[i][6] + _dot_tn(UV[i], BK[i]))
        out.append((y, Zn))
    return out


def _scan_kernel(rf, kf, vf, kkf, af, lwf, rb, kb, vb, kkb, ab, lwb, s0f, s0b, ka_ref, tri_ref, m_ref,
                 yf_o, yb_o, sf_o, sb_o, *, C, G, nchunk):
    @pl.when(pl.program_id(2) == 0)
    def _():
        sf_o[...] = s0f[...]
        sb_o[...] = s0b[...]

    nlev = int(np.log2(C))
    nmask = 2 + nlev
    lane = lax.broadcasted_iota(jnp.int32, (1, LANES), 1)
    m0 = (lane < HEAD).astype(F32)
    m1 = 1.0 - m0
    ri = lax.broadcasted_iota(jnp.int32, (C, 2 * C), 0)
    ci = lax.broadcasted_iota(jnp.int32, (C, 2 * C), 1)
    eye = (ri == jnp.bitwise_and(ci, C - 1)).astype(F32)
    ka = ka_ref[...]
    dirs = ((rf, kf, vf, kkf, af, lwf, yf_o, sf_o, False), (rb, kb, vb, kkb, ab, lwb, yb_o, sb_o, True))
    for cidx in range(nchunk):
        chains, dests = [], []
        for (r_r, k_r, v_r, kk_r, a_r, lw_r, y_o, s_o, rev) in dirs:
            c = (nchunk - 1 - cidx) if rev else cidx
            rows = slice(c * C, (c + 1) * C)
            r, k, v, kk = r_r[0, rows, :], k_r[0, rows, :], v_r[0, rows, :], kk_r[0, rows, :]
            a, lw = a_r[0, rows, :], lw_r[0, rows, :]
            hi, lo = _split(lw)
            tri = tri_ref[1 if rev else 0]
            cum = _dot(tri, hi) + _dot(tri, lo)
            ea = jnp.exp(cum - lw)
            er = jnp.exp(cum)
            einv = jnp.exp(-cum)
            At = -(kk * ea)
            Rt = r * er
            Bt = (kk * a) * einv
            Kt = (k * (1.0 + (a - 1.0) * ka)) * einv
            gam = er[0:1, :] if rev else er[C - 1:C, :]
            for g in range(G):
                sl = slice(g * LANES, (g + 1) * LANES)
                chains.append((At[:, sl], Rt[:, sl], Bt[:, sl], Kt[:, sl], v[:, sl], gam[:, sl],
                               s_o[0, g], nmask if rev else 0))
            dests.append((y_o, s_o, rows))
        res = _chunk_group(chains, m_ref, nlev, m0, m1, eye)
        for d, (y_o, s_o, rows) in enumerate(dests):
            part = res[d * G:(d + 1) * G]
            for g in range(G):
                s_o[0, g] = part[g][1]
            y_o[0, rows, :] = part[0][0] if G == 1 else jnp.concatenate([q[0] for q in part], axis=1)


def _scan_call(pre, a0, a1, lw0, lw1, s0f, s0b, ka, C, G, TB):
    r, k, v, kk = pre
    B, T, W = r.shape
    TB = min(TB, T)
    NT = T // TB
    NG = W // (G * LANES)
    GW = G * LANES
    fwd = pl.BlockSpec((1, TB, GW), lambda b, g, i: (b, i, g))
    bwd = pl.BlockSpec((1, TB, GW), lambda b, g, i: (b, NT - 1 - i, g))
    st = pl.BlockSpec((1, G, LANES, LANES), lambda b, g, i: (b, g, 0, 0))
    masks = jnp.asarray(_scan_masks(C))
    tri = jnp.asarray(np.stack([np.tril(np.ones((C, C), np.float32)),
                                np.triu(np.ones((C, C), np.float32))])).astype(BF16)
    kern = functools.partial(_scan_kernel, C=C, G=G, nchunk=TB // C)
    return pl.pallas_call(
        kern,
        grid=(B, NG, NT),
        in_specs=[fwd] * 6 + [bwd] * 6 + [st, st,
                  pl.BlockSpec((1, GW), lambda b, g, i: (0, g)),
                  pl.BlockSpec(tri.shape, lambda b, g, i: (0, 0, 0)),
                  pl.BlockSpec(masks.shape, lambda b, g, i: (0, 0, 0))],
        out_specs=[fwd, bwd, st, st],
        out_shape=[jax.ShapeDtypeStruct((B, T, W), F32)] * 2
                  + [jax.ShapeDtypeStruct(s0f.shape, F32)] * 2,
        compiler_params=_cparams(("parallel", "parallel", "arbitrary")),
        name="wkv7_scan",
    )(r, k, v, kk, a0, lw0, r, k, v, kk, a1, lw1, s0f, s0b, ka, tri, masks)


def _window_count(pos, half, n):
    return (jnp.minimum(pos + half, n) - jnp.maximum(pos - half, 0)).astype(F32)


def _pool2d_kernel(cur, prv, nxt, pw_ref, ps_ref, o_ref, col_ref, *, tp, halo, nrows):
    i = pl.program_id(1)
    has_prev = (i > 0).astype(F32)
    has_next = (i < pl.num_programs(1) - 1).astype(F32)
    PG = pw_ref.shape[1]
    shift = GRID_COLS.bit_length() - 1
    tok = lax.broadcasted_iota(jnp.int32, (tp, PG), 0)
    col = jnp.bitwise_and(tok, GRID_COLS - 1)
    row = i * (tp // GRID_COLS) + jnp.right_shift(tok, shift)
    p = lax.broadcasted_iota(jnp.int32, (LANES, LANES), 0)
    q = lax.broadcasted_iota(jnp.int32, (LANES, LANES), 1)
    same_row = jnp.right_shift(p, shift) == jnp.right_shift(q, shift)
    dcol = jnp.bitwise_and(p, GRID_COLS - 1) - jnp.bitwise_and(q, GRID_COLS - 1)
    for gi, win in enumerate(POOL_WINS):
        half = win // 2
        cs = slice(gi * PG, (gi + 1) * PG)
        band = (same_row & (dcol <= half) & (dcol > -half)).astype(BF16)

        def colsum(x):
            hi, lo = _split(x)
            return _dot(band, hi) + _dot(band, lo)

        for c in range(halo // LANES):
            rs = slice(c * LANES, (c + 1) * LANES)
            col_ref[rs, :] = colsum(prv[0, rs, cs] * has_prev)
            rs2 = slice(halo + tp + c * LANES, halo + tp + (c + 1) * LANES)
            col_ref[rs2, :] = colsum(nxt[0, rs, cs] * has_next)
        for c in range(tp // LANES):
            rs = slice(c * LANES, (c + 1) * LANES)
            col_ref[halo + c * LANES:halo + (c + 1) * LANES, :] = colsum(cur[0, rs, cs])
        acc = None
        for dr in range(-half, half):
            part = col_ref[halo + dr * GRID_COLS:halo + dr * GRID_COLS + tp, :]
            acc = part if acc is None else acc + part
        cnt = _window_count(col, half, GRID_COLS) * _window_count(row, half, nrows)
        diff = acc * (1.0 / cnt) - cur[0, :, cs]
        o_ref[0, :, cs] = _dot(diff.astype(BF16), pw_ref[gi]) * ps_ref[:, cs]


def _pool2d_call(u, pool_w, pool_scale, tp):
    B, T, _ = u.shape
    PW = pool_scale.shape[1]
    halo = POOL_HALO_ROWS * GRID_COLS
    tp = min(tp, T)
    cblk = (u.shape[2] - 2 * LORA_PAD - GATE_LORA - PW) // PW
    nh = T // halo
    kern = functools.partial(_pool2d_kernel, tp=tp, halo=halo, nrows=T // GRID_COLS)
    return pl.pallas_call(
        kern,
        grid=(B, T // tp),
        in_specs=[
            pl.BlockSpec((1, tp, PW), lambda b, i: (b, i, cblk)),
            pl.BlockSpec((1, halo, PW), lambda b, i: (b, jnp.maximum(i * (tp // halo) - 1, 0), cblk)),
            pl.BlockSpec((1, halo, PW), lambda b, i: (b, jnp.minimum((i + 1) * (tp // halo), nh - 1), cblk)),
            pl.BlockSpec(pool_w.shape, lambda b, i: (0, 0, 0)),
            pl.BlockSpec((1, PW), lambda b, i: (0, 0)),
        ],
        out_specs=pl.BlockSpec((1, tp, PW), lambda b, i: (b, i, 0)),
        out_shape=jax.ShapeDtypeStruct((B, T, PW), F32),
        scratch_shapes=[pltpu.VMEM((tp + 2 * halo, pool_w.shape[1]), F32)],
        compiler_params=_cparams(("parallel", "parallel")),
        name="pool2d",
    )(u, u, u, pool_w, pool_scale)


def _pool1d_kernel(u_ref, pw_ref, ps_ref, o_ref):
    T = u_ref.shape[1]
    PG = pw_ref.shape[1]
    p = lax.broadcasted_iota(jnp.int32, (T, T), 0)
    q = lax.broadcasted_iota(jnp.int32, (T, T), 1)
    pos = lax.broadcasted_iota(jnp.int32, (T, PG), 0)
    for gi, win in enumerate(POOL_WINS):
        half = win // 2
        cs = slice(gi * PG, (gi + 1) * PG)
        band = ((p - q <= half) & (p - q > -half)).astype(BF16)
        x = u_ref[0, :, cs]
        hi, lo = _split(x)
        m = (_dot(band, hi) + _dot(band, lo)) * (1.0 / _window_count(pos, half, T))
        o_ref[0, :, cs] = _dot((m - x).astype(BF16), pw_ref[gi]) * ps_ref[:, cs]


def _pool1d_call(u, pool_w, pool_scale):
    B, T, _ = u.shape
    PW = pool_scale.shape[1]
    cblk = (u.shape[2] - 2 * LORA_PAD - GATE_LORA - PW) // PW
    return pl.pallas_call(
        _pool1d_kernel,
        grid=(B,),
        in_specs=[
            pl.BlockSpec((1, T, PW), lambda b: (b, 0, cblk)),
            pl.BlockSpec(pool_w.shape, lambda b: (0, 0, 0)),
            pl.BlockSpec((1, PW), lambda b: (0, 0)),
        ],
        out_specs=pl.BlockSpec((1, T, PW), lambda b: (b, 0, 0)),
        out_shape=jax.ShapeDtypeStruct((B, T, PW), F32),
        compiler_params=_cparams(("parallel",)),
        name="pool1d",
    )(u, pool_w, pool_scale)


def _outproj_kernel(x_ref, yf, yb, bonus, g, p_ref, gnw, gnb, ones_ref, gt_ref, w_ref, o_ref):
    W = yf.shape[2]
    ones_bd = ones_ref[...]
    y = yf[0] + yb[0]
    mu = _head_sum(y, ones_bd) * (1.0 / HEAD)
    yc = y - mu
    var = _head_sum(yc * yc, ones_bd) * (1.0 / HEAD)
    yn = yc * lax.rsqrt(var + GN_EPS) * gnw[...] + gnb[...]
    mixed = ((yn + bonus[0]) * g[0]).astype(BF16)
    acc = _dot(mixed, w_ref[0:W, :]) + _dot(p_ref[0].astype(BF16), w_ref[W:, :])
    o_ref[0] = x_ref[0] + gt_ref[0] * acc


def _outproj_call(x, yf, yb, bonus, g, p, gnw, gnb, ones_bd, gate, w, tm):
    B, T, D = x.shape
    W, PW = yf.shape[2], p.shape[2]
    tm = min(tm, T)
    blk = lambda n: pl.BlockSpec((1, tm, n), lambda b, i: (b, i, 0))
    vec = pl.BlockSpec((1, W), lambda b, i: (0, 0))
    return pl.pallas_call(
        _outproj_kernel,
        grid=(B, T // tm),
        in_specs=[blk(D), blk(W), blk(W), blk(W), blk(W), blk(PW), vec, vec,
                  pl.BlockSpec(ones_bd.shape, lambda b, i: (0, 0)),
                  pl.BlockSpec((1, 1, D), lambda b, i: (b, 0, 0)),
                  pl.BlockSpec(w.shape, lambda b, i: (0, 0), pipeline_mode=pl.Buffered(1))],
        out_specs=blk(D),
        out_shape=jax.ShapeDtypeStruct((B, T, D), F32),
        compiler_params=_cparams(("parallel", "parallel")),
        name="mix_outproj_residual",
    )(x, yf, yb, bonus, g, p, gnw, gnb, ones_bd, gate, w)


def _ffn_kernel(x_ref, g_ref, sh_ref, sc_ref, gt_ref, wg_ref, wu_ref, wd_ref, fin_ref, o_ref, xn_ref, acc_ref,
                *, final):
    j = pl.program_id(2)

    @pl.when(j == 0)
    def _():
        xn_ref[...] = _modnorm(x_ref[0], g_ref[...], sh_ref[0], sc_ref[0]).astype(BF16)
        acc_ref[...] = jnp.zeros_like(acc_ref)

    xn = xn_ref[...]
    h = _silu(_dot(xn, wg_ref[...])) * _dot(xn, wu_ref[...])
    acc_ref[...] += _dot(h.astype(BF16), wd_ref[...])

    @pl.when(j == pl.num_programs(2) - 1)
    def _():
        h = x_ref[0] + gt_ref[0] * acc_ref[...]
        if final:
            h = h * lax.rsqrt(jnp.mean(h * h, axis=-1, keepdims=True) + RMS_EPS) * fin_ref[...]
        o_ref[0] = h


def _ffn_call(x, g, shift, scale, gate, wg, wu, wd, fin, final, tm, tf):
    B, T, D = x.shape
    F = wg.shape[1]
    tm = min(tm, T)
    mod = pl.BlockSpec((1, 1, D), lambda b, i, j: (b, 0, 0))
    return pl.pallas_call(
        functools.partial(_ffn_kernel, final=final),
        grid=(B, T // tm, F // tf),
        in_specs=[
            pl.BlockSpec((1, tm, D), lambda b, i, j: (b, i, 0)),
            pl.BlockSpec((1, D), lambda b, i, j: (0, 0)),
            mod, mod, mod,
            pl.BlockSpec((D, tf), lambda b, i, j: (0, j)),
            pl.BlockSpec((D, tf), lambda b, i, j: (0, j)),
            pl.BlockSpec((tf, D), lambda b, i, j: (j, 0)),
            pl.BlockSpec((1, D), lambda b, i, j: (0, 0)),
        ],
        out_specs=pl.BlockSpec((1, tm, D), lambda b, i, j: (b, i, 0)),
        out_shape=jax.ShapeDtypeStruct((B, T, D), F32),
        scratch_shapes=[pltpu.VMEM((tm, D), BF16), pltpu.VMEM((tm, D), F32)],
        compiler_params=_cparams(("parallel", "parallel", "arbitrary")),
        name="norm_swiglu_residual",
    )(x, g, shift, scale, gate, wg, wu, wd, fin)


def _pad_rows(w, n):
    return jnp.pad(w, ((0, 0),) * (w.ndim - 2) + ((0, n - w.shape[-2]), (0, 0)))


def _layer_params(l, w_in, conv_rkv, decay_bias, decay_up, iclr_bias, iclr_up, gate_up, k_k, k_a, r_k,
                  gn_w, gn_b, pool_w, pool_scale, w_out, ffn_gate, ffn_up, ffn_down):
    RW = k_k.shape[1]
    wl = w_in[l]
    o_w = 3 * RW
    o_a = o_w + DECAY_LORA
    o_g = o_a + ICLR_LORA
    o_p = o_g + GATE_LORA
    padc = lambda w: jnp.pad(w, ((0, 0), (0, LORA_PAD - w.shape[1])))
    w_in_r = jnp.concatenate(
        [wl[:, :o_w], wl[:, o_p:], padc(wl[:, o_w:o_a]), padc(wl[:, o_a:o_g]), wl[:, o_g:o_p]], axis=1)
    row = lambda a: a[l].reshape(1, -1)
    return {
        "w_in": w_in_r.astype(BF16),
        "conv": conv_rkv[l],
        "dbias": decay_bias[l], "dup": _pad_rows(decay_up[l], LORA_PAD).astype(BF16),
        "ibias": iclr_bias[l], "iup": _pad_rows(iclr_up[l], LORA_PAD).astype(BF16),
        "gup": gate_up[l].astype(BF16),
        "k_k": row(k_k), "k_a": row(k_a), "r_k": row(r_k), "gn_w": row(gn_w), "gn_b": row(gn_b),
        "pool_w": pool_w[l].astype(BF16), "pool_scale": row(pool_scale),
        "w_out": w_out[l].astype(BF16),
        "wg": ffn_gate[l].astype(BF16), "wu": ffn_up[l].astype(BF16), "wd": ffn_down[l].astype(BF16),
    }


_CFG = dict(tm_in=512, tt_prep=256, scan_c=64, scan_g=8, scan_tb=64, tp_pool=2048, tm_out=512,
            tm_ffn=512, tf_ffn=512, tn_ada=1024)


def _forward(cfg, x, c, ctx, c_ctx, ada_w, ada_b, norm_mix, norm_ffn, w_in, conv_rkv, decay_bias, decay_up,
             iclr_bias, iclr_up, gate_up, k_k, k_a, r_k, gn_w, gn_b, pool_w, pool_scale, w_out,
             ffn_gate, ffn_up, ffn_down, final_norm):
    B, T, D = x.shape
    L = ada_w.shape[0]
    RW = k_k.shape[1]
    npair = RW // LANES
    ones_bd = jnp.asarray(np.kron(np.eye(LANES // HEAD), np.ones((HEAD, HEAD))), BF16)
    fin = final_norm.reshape(1, D)

    cond = jnp.zeros((8, D), F32).at[:B].set(c).at[B].set(c_ctx)
    mods = _ada_call(cond, ada_w, ada_b, cfg["tn_ada"])
    mods = mods.reshape(L, 8, 6, D)

    ctx_h = ctx
    for l in range(L):
        last = l == L - 1
        p = _layer_params(l, w_in, conv_rkv, decay_bias, decay_up, iclr_bias, iclr_up, gate_up, k_k, k_a,
                          r_k, gn_w, gn_b, pool_w, pool_scale, w_out, ffn_gate, ffn_up, ffn_down)
        p["ones_bd"] = ones_bd
        m_lat = [mods[l, :B, s].reshape(B, 1, D) for s in range(6)]
        m_ctx = [jnp.broadcast_to(mods[l, B, s].reshape(1, 1, D), (B, 1, D)) for s in range(6)]
        g_mix = norm_mix[l].reshape(1, D)
        g_ffn = norm_ffn[l].reshape(1, D)
        zero = jnp.zeros((B, npair, LANES, LANES), F32)

        def mixer(h, m, s0f, s0b, need_out):
            u = _inproj_call(h, g_mix, m[0], m[1], p["w_in"], cfg["tm_in"])
            r, k, v, kk, a0, a1, lw0, lw1, g, bonus = _prep_call(u, p, cfg["tt_prep"])
            yf, yb, sf, sb = _scan_call((r, k, v, kk), a0, a1, lw0, lw1, s0f, s0b, p["k_a"],
                                        cfg["scan_c"], cfg["scan_g"], cfg["scan_tb"])
            return u, (yf, yb, bonus, g), sf, sb

        def residuals(h, m, y, pooled, final):
            h = _outproj_call(h, *y, pooled, p["gn_w"], p["gn_b"], ones_bd, m[2], p["w_out"], cfg["tm_out"])
            return _ffn_call(h, g_ffn, m[3], m[4], m[5], p["wg"], p["wu"], p["wd"], fin, final,
                             cfg["tm_ffn"], cfg["tf_ffn"])

        u_ctx, y_ctx, s_fwd, s_bwd = mixer(ctx_h, m_ctx, zero, zero, not last)
        u_lat, y_lat, _, _ = mixer(x, m_lat, s_fwd, s_bwd, True)
        p_lat = _pool2d_call(u_lat, p["pool_w"], p["pool_scale"], cfg["tp_pool"])
        x = residuals(x, m_lat, y_lat, p_lat, last)
        if not last:
            p_ctx = _pool1d_call(u_ctx, p["pool_w"], p["pool_scale"])
            ctx_h = residuals(ctx_h, m_ctx, y_ctx, p_ctx, False)
    return x


def kernel(x, c, ctx, c_ctx, ada_w, ada_b, norm_mix, norm_ffn, w_in, conv_rkv, decay_bias, decay_up,
           iclr_bias, iclr_up, gate_up, k_k, k_a, r_k, gn_w, gn_b, pool_w, pool_scale, w_out,
           ffn_gate, ffn_up, ffn_down, final_norm):
    return _forward(_CFG, x, c, ctx, c_ctx, ada_w, ada_b, norm_mix, norm_ffn, w_in, conv_rkv, decay_bias,
                    decay_up, iclr_bias, iclr_up, gate_up, k_k, k_a, r_k, gn_w, gn_b, pool_w, pool_scale,
                    w_out, ffn_gate, ffn_up, ffn_down, final_norm)
```

```python
import functools

import numpy as np
import jax
import jax.numpy as jnp
from jax import lax
from jax.experimental import pallas as pl
from jax.experimental.pallas import tpu as pltpu

F32 = jnp.float32
BF16 = jnp.bfloat16

HEAD = 64
LANES = 128
GRID_COLS = 64
POOL_WINS = (2, 4, 8, 16)
POOL_HALO_ROWS = 8
RMS_EPS = 1e-6
GN_EPS = 64e-5
DECAY_LORA = 96
ICLR_LORA = 96
GATE_LORA = 256
LORA_PAD = 128
VMEM_LIMIT = 56 * 1024 * 1024


def _cparams(sem):
    return pltpu.CompilerParams(dimension_semantics=sem, vmem_limit_bytes=VMEM_LIMIT)


def _dot(a, b):
    return jnp.dot(a, b, preferred_element_type=F32)


def _dot_nt(a, b):
    return lax.dot_general(a, b, (((1,), (1,)), ((), ())), preferred_element_type=F32)


def _dot_tn(a, b):
    return lax.dot_general(a, b, (((0,), (0,)), ((), ())), preferred_element_type=F32)


def _split(x):
    hi = x.astype(BF16)
    lo = (x - hi.astype(F32)).astype(BF16)
    return hi, lo


def _head_sum(x, ones_bd):
    hi, lo = _split(x)
    parts = []
    for c in range(x.shape[1] // LANES):
        sl = slice(c * LANES, (c + 1) * LANES)
        parts.append(_dot(hi[:, sl], ones_bd) + _dot(lo[:, sl], ones_bd))
    return parts[0] if len(parts) == 1 else jnp.concatenate(parts, axis=1)


def _sigmoid(x):
    return 1.0 / (1.0 + jnp.exp(-x))


def _silu(x):
    return x * _sigmoid(x)


def _ada_kernel(c_ref, w_ref, b_ref, o_ref):
    c = c_ref[...]
    o_ref[0] = _dot(_silu(c).astype(BF16), w_ref[0].astype(BF16)) + b_ref[0]


def _ada_call(cond, ada_w, ada_b, tn):
    L, D, N = ada_w.shape
    R = cond.shape[0]
    return pl.pallas_call(
        _ada_kernel,
        grid=(L, N // tn),
        in_specs=[
            pl.BlockSpec((R, D), lambda l, j: (0, 0)),
            pl.BlockSpec((1, D, tn), lambda l, j: (l, 0, j)),
            pl.BlockSpec((1, 1, tn), lambda l, j: (l, 0, j)),
        ],
        out_specs=pl.BlockSpec((1, R, tn), lambda l, j: (l, 0, j)),
        out_shape=jax.ShapeDtypeStruct((L, R, N), F32),
        compiler_params=_cparams(("parallel", "parallel")),
        name="ada_modulation",
    )(cond, ada_w, ada_b.reshape(L, 1, N))


def _modnorm(x, g, shift, scale):
    ms = jnp.mean(x * x, axis=-1, keepdims=True)
    return (x * lax.rsqrt(ms + RMS_EPS) * g) * (1.0 + scale) + shift


HALO = 8


def _inprep_kernel(x_ref, xp_ref, xn_ref, g_ref, sh_ref, sc_ref, w_ref, cw, dbias, dup, ibias, iup, gup,
                   kkw, ka, rk, ones_ref,
                   up_o, r_o, k_o, v_o, kk_o, a0_o, a1_o, lw0_o, lw1_o, g_o, bonus_o):
    i = pl.program_id(1)
    has_prev = (i > 0).astype(F32)
    has_next = (i < pl.num_programs(1) - 1).astype(F32)
    tm = x_ref.shape[1]
    W = r_o.shape[2]
    ones_bd = ones_ref[...]
    norm = lambda x: _modnorm(x, g_ref[...], sh_ref[0], sc_ref[0])
    xn = jnp.concatenate([norm(xp_ref[0]) * has_prev, norm(x_ref[0]), norm(xn_ref[0]) * has_next], axis=0)
    u = _dot(xn.astype(BF16), w_ref[...])
    rows = slice(HALO, HALO + tm)
    rkv = u[:, 0:3 * W]
    conv = (pltpu.roll(rkv, 1, 0)[rows] * cw[0:1, :] + rkv[rows] * cw[1:2, :]
            + pltpu.roll(rkv, tm + 2 * HALO - 1, 0)[rows] * cw[2:3, :])
    r, k, v = conv[:, 0:W], conv[:, W:2 * W], conv[:, 2 * W:3 * W]
    up_o[0] = u[rows, 3 * W:3 * W + up_o.shape[2]]
    r_o[0] = r.astype(r_o.dtype)
    k_o[0] = k.astype(k_o.dtype)
    v_o[0] = v.astype(v_o.dtype)
    kk = k * kkw[...]
    kk = kk * lax.rsqrt(jnp.maximum(_head_sum(kk * kk, ones_bd), 1e-24))
    kk_o[0] = kk.astype(kk_o.dtype)
    lo = u[rows, u.shape[1] - 2 * LORA_PAD - GATE_LORA:]
    w_lo = jnp.tanh(lo[:, 0:LORA_PAD]).astype(BF16)
    a_lo = lo[:, LORA_PAD:2 * LORA_PAD].astype(BF16)
    g_lo = _sigmoid(lo[:, 2 * LORA_PAD:2 * LORA_PAD + GATE_LORA]).astype(BF16)
    a_sum = None
    for d, (a_o, lw_o) in enumerate(((a0_o, lw0_o), (a1_o, lw1_o))):
        z = dbias[d:d + 1, :] + _dot(w_lo, dup[d])
        w_log = jnp.minimum(z, 0.0) - jnp.log(1.0 + jnp.exp(-jnp.abs(z))) - 0.5
        lw_o[0] = -jnp.exp(w_log)
        a = _sigmoid(ibias[d:d + 1, :] + _dot(a_lo, iup[d]))
        a_o[0] = a.astype(a_o.dtype)
        a_sum = a if a_sum is None else a_sum + a
    g_o[0] = _dot(g_lo, gup[...])
    kpair = k * (2.0 + (a_sum - 2.0) * ka[...])
    bonus_o[0] = _head_sum(r * kpair * rk[...], ones_bd) * v


def _inprep_call(x, g, shift, scale, p, tm):
    B, T, D = x.shape
    W = p["k_k"].shape[1]
    PW = p["pool_scale"].shape[1]
    tm = min(tm, T)
    nb = T // HALO
    halo = lambda f: pl.BlockSpec((1, HALO, D), lambda b, i: (b, f(i), 0))
    full = lambda a, **kw: pl.BlockSpec(a.shape, lambda b, i: (0,) * a.ndim, **kw)
    mod = pl.BlockSpec((1, 1, D), lambda b, i: (b, 0, 0))
    consts = [p["conv"], p["dbias"], p["dup"], p["ibias"], p["iup"], p["gup"],
              p["k_k"], p["k_a"], p["r_k"], p["ones_bd"]]
    out_blk = lambda n: pl.BlockSpec((1, tm, n), lambda b, i: (b, i, 0))
    sds = lambda n, dt: jax.ShapeDtypeStruct((B, T, n), dt)
    return pl.pallas_call(
        _inprep_kernel,
        grid=(B, T // tm),
        in_specs=[pl.BlockSpec((1, tm, D), lambda b, i: (b, i, 0)),
                  halo(lambda i: jnp.maximum(i * (tm // HALO) - 1, 0)),
                  halo(lambda i: jnp.minimum((i + 1) * (tm // HALO), nb - 1)),
                  pl.BlockSpec((1, D), lambda b, i: (0, 0)), mod, mod,
                  full(p["w_in"], pipeline_mode=pl.Buffered(1))]
                 + [full(a) for a in consts],
        out_specs=[out_blk(PW)] + [out_blk(W)] * 10,
        out_shape=[sds(PW, F32)] + [sds(W, BF16)] * 6 + [sds(W, F32)] * 4,
        compiler_params=_cparams(("parallel", "parallel")),
        name="norm_inproj_prep",
    )(x, x, x, g, shift, scale, p["w_in"], *consts)


def _scan_masks(C):
    t = np.arange(C)[:, None]
    j = np.arange(2 * C)[None, :] % C
    fwd = [t > j, t >= j]
    rev = [t < j, t <= j]
    s = 1
    while s < C:
        blk = (t // (2 * s)) == (j // (2 * s))
        fwd.append(blk & ((t % (2 * s)) >= s) & ((j % (2 * s)) < s))
        rev.append(blk & ((t % (2 * s)) < s) & ((j % (2 * s)) >= s))
        s *= 2
    return np.stack(fwd + rev).astype(np.float32)


def _chunk_group(chains, masks, nlev, m0, m1, eye):
    n = len(chains)
    C = chains[0][0].shape[0]
    P = 2 * C
    stack = lambda xb: jnp.concatenate([xb * m0, xb * m1], axis=0)
    AR, BK, Vs = [], [], []
    for (At, Rt, Bt, Kt, v, _, _, _) in chains:
        AR.append(jnp.concatenate([At, Rt], axis=0).astype(BF16))
        BK.append(jnp.concatenate([stack(Bt.astype(BF16)), stack(Kt.astype(BF16))], axis=0))
        Vs.append(stack(v.astype(BF16)))
    sc = [_dot_nt(AR[i], BK[i]).astype(BF16) for i in range(n)]
    x1 = [_dot_nt(AR[i], chains[i][6].astype(BF16)) for i in range(n)]
    Nab = [sc[i][:C, :P] * masks[chains[i][7]] for i in range(n)]
    T = [eye + (Nab[i] * masks[chains[i][7] + 2]).astype(F32) for i in range(n)]
    for lv in range(1, nlev):
        Tb = [T[i].astype(BF16) for i in range(n)]
        Nl = [stack(Nab[i] * masks[chains[i][7] + 2 + lv]) for i in range(n)]
        TN = [_dot(Tb[i], Nl[i]).astype(BF16) for i in range(n)]
        T = [T[i] + _dot(TN[i], stack(Tb[i])) for i in range(n)]
    NV = [_dot(sc[i][:C, P:] * masks[chains[i][7]], Vs[i]) for i in range(n)]
    Us = [_dot(T[i].astype(BF16), stack((x1[i][:C] + NV[i]).astype(BF16))) for i in range(n)]
    UV = [jnp.concatenate([stack(Us[i].astype(BF16)), Vs[i]], axis=0) for i in range(n)]
    out = []
    for i in range(n):
        incl = masks[chains[i][7] + 1]
        Pr = jnp.concatenate([sc[i][C:, :P] * incl, sc[i][C:, P:] * incl], axis=1)
        y = x1[i][C:] + _dot(Pr, UV[i])
        Zn = chains[i][5] * (chains[i][6] + _dot_tn(UV[i], BK[i]))
        out.append((y, Zn))
    return out


def _scan_kernel(rf, kf, vf, kkf, af, lwf, rb, kb, vb, kkb, ab, lwb, s0f, s0b, ka_ref, tri_ref, m_ref,
                 yf_o, yb_o, sf_o, sb_o, *, C, G, nchunk):
    @pl.when(pl.program_id(2) == 0)
    def _():
        sf_o[...] = s0f[...]
        sb_o[...] = s0b[...]

    nlev = int(np.log2(C))
    nmask = 2 + nlev
    lane = lax.broadcasted_iota(jnp.int32, (1, LANES), 1)
    m0 = (lane < HEAD).astype(BF16)
    m1 = (lane >= HEAD).astype(BF16)
    ri = lax.broadcasted_iota(jnp.int32, (C, 2 * C), 0)
    ci = lax.broadcasted_iota(jnp.int32, (C, 2 * C), 1)
    eye = (ri == jnp.bitwise_and(ci, C - 1)).astype(F32)
    ka = ka_ref[...]
    dirs = ((rf, kf, vf, kkf, af, lwf, yf_o, sf_o, False), (rb, kb, vb, kkb, ab, lwb, yb_o, sb_o, True))
    for cidx in range(nchunk):
        chains, dests = [], []
        for (r_r, k_r, v_r, kk_r, a_r, lw_r, y_o, s_o, rev) in dirs:
            c = (nchunk - 1 - cidx) if rev else cidx
            rows = slice(c * C, (c + 1) * C)
            r, k, kk, a = (z[0, rows, :].astype(F32) for z in (r_r, k_r, kk_r, a_r))
            v, lw = v_r[0, rows, :], lw_r[0, rows, :]
            hi, lo = _split(lw)
            tri = tri_ref[1 if rev else 0]
            cum = _dot(tri, hi) + _dot(tri, lo)
            ea = jnp.exp(cum - lw)
            er = jnp.exp(cum)
            einv = jnp.exp(-cum)
            At = -(kk * ea)
            Rt = r * er
            Bt = (kk * a) * einv
            Kt = (k * (1.0 + (a - 1.0) * ka)) * einv
            gam = er[0:1, :] if rev else er[C - 1:C, :]
            for g in range(G):
                sl = slice(g * LANES, (g + 1) * LANES)
                chains.append((At[:, sl], Rt[:, sl], Bt[:, sl], Kt[:, sl], v[:, sl], gam[:, sl],
                               s_o[0, g], nmask if rev else 0))
            dests.append((y_o, s_o, rows))
        res = _chunk_group(chains, m_ref, nlev, m0, m1, eye)
        for d, (y_o, s_o, rows) in enumerate(dests):
            part = res[d * G:(d + 1) * G]
            for g in range(G):
                s_o[0, g] = part[g][1]
            y_o[0, rows, :] = part[0][0] if G == 1 else jnp.concatenate([q[0] for q in part], axis=1)


def _scan_call(pre, a0, a1, lw0, lw1, s0f, s0b, ka, C, G, TB):
    r, k, v, kk = pre
    B, T, W = r.shape
    TB = min(TB, T)
    NT = T // TB
    NG = W // (G * LANES)
    GW = G * LANES
    fwd = pl.BlockSpec((1, TB, GW), lambda b, g, i: (b, i, g))
    bwd = pl.BlockSpec((1, TB, GW), lambda b, g, i: (b, NT - 1 - i, g))
    st = pl.BlockSpec((1, G, LANES, LANES), lambda b, g, i: (b, g, 0, 0))
    masks = jnp.asarray(_scan_masks(C), BF16)
    tri = jnp.asarray(np.stack([np.tril(np.ones((C, C), np.float32)),
                                np.triu(np.ones((C, C), np.float32))])).astype(BF16)
    kern = functools.partial(_scan_kernel, C=C, G=G, nchunk=TB // C)
    return pl.pallas_call(
        kern,
        grid=(B, NG, NT),
        in_specs=[fwd] * 6 + [bwd] * 6 + [st, st,
                  pl.BlockSpec((1, GW), lambda b, g, i: (0, g)),
                  pl.BlockSpec(tri.shape, lambda b, g, i: (0, 0, 0)),
                  pl.BlockSpec(masks.shape, lambda b, g, i: (0, 0, 0))],
        out_specs=[fwd, bwd, st, st],
        out_shape=[jax.ShapeDtypeStruct((B, T, W), F32)] * 2
                  + [jax.ShapeDtypeStruct(s0f.shape, F32)] * 2,
        compiler_params=_cparams(("parallel", "parallel", "arbitrary")),
        name="wkv7_scan",
    )(r, k, v, kk, a0, lw0, r, k, v, kk, a1, lw1, s0f, s0b, ka, tri, masks)


def _window_count(pos, half, n):
    return (jnp.minimum(pos + half, n) - jnp.maximum(pos - half, 0)).astype(F32)


def _pool2d_kernel(cur, prv, nxt, pw_ref, ps_ref, o_ref, col_ref, *, tp, halo, nrows):
    i = pl.program_id(1)
    has_prev = (i > 0).astype(F32)
    has_next = (i < pl.num_programs(1) - 1).astype(F32)
    PG = pw_ref.shape[1]
    shift = GRID_COLS.bit_length() - 1
    tok = lax.broadcasted_iota(jnp.int32, (tp, PG), 0)
    col = jnp.bitwise_and(tok, GRID_COLS - 1)
    row = i * (tp // GRID_COLS) + jnp.right_shift(tok, shift)
    p = lax.broadcasted_iota(jnp.int32, (LANES, LANES), 0)
    q = lax.broadcasted_iota(jnp.int32, (LANES, LANES), 1)
    same_row = jnp.right_shift(p, shift) == jnp.right_shift(q, shift)
    dcol = jnp.bitwise_and(p, GRID_COLS - 1) - jnp.bitwise_and(q, GRID_COLS - 1)
    for gi, win in enumerate(POOL_WINS):
        half = win // 2
        cs = slice(gi * PG, (gi + 1) * PG)
        band = (same_row & (dcol <= half) & (dcol > -half)).astype(BF16)

        def colsum(x):
            hi, lo = _split(x)
            return _dot(band, hi) + _dot(band, lo)

        for c in range(halo // LANES):
            rs = slice(c * LANES, (c + 1) * LANES)
            col_ref[rs, :] = colsum(prv[0, rs, cs] * has_prev)
            rs2 = slice(halo + tp + c * LANES, halo + tp + (c + 1) * LANES)
            col_ref[rs2, :] = colsum(nxt[0, rs, cs] * has_next)
        for c in range(tp // LANES):
            rs = slice(c * LANES, (c + 1) * LANES)
            col_ref[halo + c * LANES:halo + (c + 1) * LANES, :] = colsum(cur[0, rs, cs])
        acc = None
        for dr in range(-half, half):
            part = col_ref[halo + dr * GRID_COLS:halo + dr * GRID_COLS + tp, :]
            acc = part if acc is None else acc + part
        cnt = _window_count(col, half, GRID_COLS) * _window_count(row, half, nrows)
        diff = acc * (1.0 / cnt) - cur[0, :, cs]
        o_ref[0, :, cs] = _dot(diff.astype(BF16), pw_ref[gi]) * ps_ref[:, cs]


def _pool2d_call(u, pool_w, pool_scale, tp):
    B, T, _ = u.shape
    PW = pool_scale.shape[1]
    halo = POOL_HALO_ROWS * GRID_COLS
    tp = min(tp, T)
    nh = T // halo
    kern = functools.partial(_pool2d_kernel, tp=tp, halo=halo, nrows=T // GRID_COLS)
    return pl.pallas_call(
        kern,
        grid=(B, T // tp),
        in_specs=[
            pl.BlockSpec((1, tp, PW), lambda b, i: (b, i, 0)),
            pl.BlockSpec((1, halo, PW), lambda b, i: (b, jnp.maximum(i * (tp // halo) - 1, 0), 0)),
            pl.BlockSpec((1, halo, PW), lambda b, i: (b, jnp.minimum((i + 1) * (tp // halo), nh - 1), 0)),
            pl.BlockSpec(pool_w.shape, lambda b, i: (0, 0, 0)),
            pl.BlockSpec((1, PW), lambda b, i: (0, 0)),
        ],
        out_specs=pl.BlockSpec((1, tp, PW), lambda b, i: (b, i, 0)),
        out_shape=jax.ShapeDtypeStruct((B, T, PW), F32),
        scratch_shapes=[pltpu.VMEM((tp + 2 * halo, pool_w.shape[1]), F32)],
        compiler_params=_cparams(("parallel", "parallel")),
        name="pool2d",
    )(u, u, u, pool_w, pool_scale)


def _pool1d_kernel(u_ref, pw_ref, ps_ref, o_ref):
    T = u_ref.shape[1]
    PG = pw_ref.shape[1]
    p = lax.broadcasted_iota(jnp.int32, (T, T), 0)
    q = lax.broadcasted_iota(jnp.int32, (T, T), 1)
    pos = lax.broadcasted_iota(jnp.int32, (T, PG), 0)
    for gi, win in enumerate(POOL_WINS):
        half = win // 2
        cs = slice(gi * PG, (gi + 1) * PG)
        band = ((p - q <= half) & (p - q > -half)).astype(BF16)
        x = u_ref[0, :, cs]
        hi, lo = _split(x)
        m = (_dot(band, hi) + _dot(band, lo)) * (1.0 / _window_count(pos, half, T))
        o_ref[0, :, cs] = _dot((m - x).astype(BF16), pw_ref[gi]) * ps_ref[:, cs]


def _pool1d_call(u, pool_w, pool_scale):
    B, T, _ = u.shape
    PW = pool_scale.shape[1]
    return pl.pallas_call(
        _pool1d_kernel,
        grid=(B,),
        in_specs=[
            pl.BlockSpec((1, T, PW), lambda b: (b, 0, 0)),
            pl.BlockSpec(pool_w.shape, lambda b: (0, 0, 0)),
            pl.BlockSpec((1, PW), lambda b: (0, 0)),
        ],
        out_specs=pl.BlockSpec((1, T, PW), lambda b: (b, 0, 0)),
        out_shape=jax.ShapeDtypeStruct((B, T, PW), F32),
        compiler_params=_cparams(("parallel",)),
        name="pool1d",
    )(u, pool_w, pool_scale)


def _outproj_kernel(x_ref, yf, yb, bonus, g, p_ref, gnw, gnb, ones_ref, gt_ref, w_ref, o_ref):
    W = yf.shape[2]
    ones_bd = ones_ref[...]
    y = yf[0] + yb[0]
    mu = _head_sum(y, ones_bd) * (1.0 / HEAD)
    yc = y - mu
    var = _head_sum(yc * yc, ones_bd) * (1.0 / HEAD)
    yn = yc * lax.rsqrt(var + GN_EPS) * gnw[...] + gnb[...]
    mixed = ((yn + bonus[0]) * g[0]).astype(BF16)
    acc = _dot(mixed, w_ref[0:W, :]) + _dot(p_ref[0].astype(BF16), w_ref[W:, :])
    o_ref[0] = x_ref[0] + gt_ref[0] * acc


def _outproj_call(x, yf, yb, bonus, g, p, gnw, gnb, ones_bd, gate, w, tm):
    B, T, D = x.shape
    W, PW = yf.shape[2], p.shape[2]
    tm = min(tm, T)
    blk = lambda n: pl.BlockSpec((1, tm, n), lambda b, i: (b, i, 0))
    vec = pl.BlockSpec((1, W), lambda b, i: (0, 0))
    return pl.pallas_call(
        _outproj_kernel,
        grid=(B, T // tm),
        in_specs=[blk(D), blk(W), blk(W), blk(W), blk(W), blk(PW), vec, vec,
                  pl.BlockSpec(ones_bd.shape, lambda b, i: (0, 0)),
                  pl.BlockSpec((1, 1, D), lambda b, i: (b, 0, 0)),
                  pl.BlockSpec(w.shape, lambda b, i: (0, 0), pipeline_mode=pl.Buffered(1))],
        out_specs=blk(D),
        out_shape=jax.ShapeDtypeStruct((B, T, D), F32),
        compiler_params=_cparams(("parallel", "parallel")),
        name="mix_outproj_residual",
    )(x, yf, yb, bonus, g, p, gnw, gnb, ones_bd, gate, w)


def _ffn_kernel(x_ref, g_ref, sh_ref, sc_ref, gt_ref, wg_ref, wu_ref, wd_ref, fin_ref, o_ref, xn_ref, acc_ref,
                *, final):
    j = pl.program_id(2)

    @pl.when(j == 0)
    def _():
        xn_ref[...] = _modnorm(x_ref[0], g_ref[...], sh_ref[0], sc_ref[0]).astype(BF16)
        acc_ref[...] = jnp.zeros_like(acc_ref)

    xn = xn_ref[...]
    h = _silu(_dot(xn, wg_ref[...])) * _dot(xn, wu_ref[...])
    acc_ref[...] += _dot(h.astype(BF16), wd_ref[...])

    @pl.when(j == pl.num_programs(2) - 1)
    def _():
        h = x_ref[0] + gt_ref[0] * acc_ref[...]
        if final:
            h = h * lax.rsqrt(jnp.mean(h * h, axis=-1, keepdims=True) + RMS_EPS) * fin_ref[...]
        o_ref[0] = h


def _ffn_call(x, g, shift, scale, gate, wg, wu, wd, fin, final, tm, tf):
    B, T, D = x.shape
    F = wg.shape[1]
    tm = min(tm, T)
    mod = pl.BlockSpec((1, 1, D), lambda b, i, j: (b, 0, 0))
    return pl.pallas_call(
        functools.partial(_ffn_kernel, final=final),
        grid=(B, T // tm, F // tf),
        in_specs=[
            pl.BlockSpec((1, tm, D), lambda b, i, j: (b, i, 0)),
            pl.BlockSpec((1, D), lambda b, i, j: (0, 0)),
            mod, mod, mod,
            pl.BlockSpec((D, tf), lambda b, i, j: (0, j)),
            pl.BlockSpec((D, tf), lambda b, i, j: (0, j)),
            pl.BlockSpec((tf, D), lambda b, i, j: (j, 0)),
            pl.BlockSpec((1, D), lambda b, i, j: (0, 0)),
        ],
        out_specs=pl.BlockSpec((1, tm, D), lambda b, i, j: (b, i, 0)),
        out_shape=jax.ShapeDtypeStruct((B, T, D), F32),
        scratch_shapes=[pltpu.VMEM((tm, D), BF16), pltpu.VMEM((tm, D), F32)],
        compiler_params=_cparams(("parallel", "parallel", "arbitrary")),
        name="norm_swiglu_residual",
    )(x, g, shift, scale, gate, wg, wu, wd, fin)


def _pad_rows(w, n):
    return jnp.pad(w, ((0, 0),) * (w.ndim - 2) + ((0, n - w.shape[-2]), (0, 0)))


def _layer_params(l, w_in, conv_rkv, decay_bias, decay_up, iclr_bias, iclr_up, gate_up, k_k, k_a, r_k,
                  gn_w, gn_b, pool_w, pool_scale, w_out, ffn_gate, ffn_up, ffn_down):
    RW = k_k.shape[1]
    wl = w_in[l]
    o_w = 3 * RW
    o_a = o_w + DECAY_LORA
    o_g = o_a + ICLR_LORA
    o_p = o_g + GATE_LORA
    padc = lambda w: jnp.pad(w, ((0, 0), (0, LORA_PAD - w.shape[1])))
    w_in_r = jnp.concatenate(
        [wl[:, :o_w], wl[:, o_p:], padc(wl[:, o_w:o_a]), padc(wl[:, o_a:o_g]), wl[:, o_g:o_p]], axis=1)
    row = lambda a: a[l].reshape(1, -1)
    return {
        "w_in": w_in_r.astype(BF16),
        "conv": conv_rkv[l],
        "dbias": decay_bias[l], "dup": _pad_rows(decay_up[l], LORA_PAD).astype(BF16),
        "ibias": iclr_bias[l], "iup": _pad_rows(iclr_up[l], LORA_PAD).astype(BF16),
        "gup": gate_up[l].astype(BF16),
        "k_k": row(k_k), "k_a": row(k_a), "r_k": row(r_k), "gn_w": row(gn_w), "gn_b": row(gn_b),
        "pool_w": pool_w[l].astype(BF16), "pool_scale": row(pool_scale),
        "w_out": w_out[l].astype(BF16),
        "wg": ffn_gate[l].astype(BF16), "wu": ffn_up[l].astype(BF16), "wd": ffn_down[l].astype(BF16),
    }


_CFG = dict(tm_in=256, scan_c=64, scan_g=8, scan_tb=64, tp_pool=2048, tm_out=512,
            tm_ffn=512, tf_ffn=512, tn_ada=1024)


def _forward(cfg, x, c, ctx, c_ctx, ada_w, ada_b, norm_mix, norm_ffn, w_in, conv_rkv, decay_bias, decay_up,
             iclr_bias, iclr_up, gate_up, k_k, k_a, r_k, gn_w, gn_b, pool_w, pool_scale, w_out,
             ffn_gate, ffn_up, ffn_down, final_norm):
    B, T, D = x.shape
    L = ada_w.shape[0]
    RW = k_k.shape[1]
    npair = RW // LANES
    ones_bd = jnp.asarray(np.kron(np.eye(LANES // HEAD), np.ones((HEAD, HEAD))), BF16)
    fin = final_norm.reshape(1, D)

    cond = jnp.zeros((8, D), F32).at[:B].set(c).at[B].set(c_ctx)
    mods = _ada_call(cond, ada_w, ada_b, cfg["tn_ada"])
    mods = mods.reshape(L, 8, 6, D)

    ctx_h = ctx
    for l in range(L):
        last = l == L - 1
        p = _layer_params(l, w_in, conv_rkv, decay_bias, decay_up, iclr_bias, iclr_up, gate_up, k_k, k_a,
                          r_k, gn_w, gn_b, pool_w, pool_scale, w_out, ffn_gate, ffn_up, ffn_down)
        p["ones_bd"] = ones_bd
        m_lat = [mods[l, :B, s].reshape(B, 1, D) for s in range(6)]
        m_ctx = [jnp.broadcast_to(mods[l, B, s].reshape(1, 1, D), (B, 1, D)) for s in range(6)]
        g_mix = norm_mix[l].reshape(1, D)
        g_ffn = norm_ffn[l].reshape(1, D)
        zero = jnp.zeros((B, npair, LANES, LANES), F32)

        def mixer(h, m, s0f, s0b, need_out):
            u, r, k, v, kk, a0, a1, lw0, lw1, g, bonus = _inprep_call(h, g_mix, m[0], m[1], p, cfg["tm_in"])
            yf, yb, sf, sb = _scan_call((r, k, v, kk), a0, a1, lw0, lw1, s0f, s0b, p["k_a"],
                                        cfg["scan_c"], cfg["scan_g"], cfg["scan_tb"])
            return u, (yf, yb, bonus, g), sf, sb

        def residuals(h, m, y, pooled, final):
            h = _outproj_call(h, *y, pooled, p["gn_w"], p["gn_b"], ones_bd, m[2], p["w_out"], cfg["tm_out"])
            return _ffn_call(h, g_ffn, m[3], m[4], m[5], p["wg"], p["wu"], p["wd"], fin, final,
                             cfg["tm_ffn"], cfg["tf_ffn"])

        u_ctx, y_ctx, s_fwd, s_bwd = mixer(ctx_h, m_ctx, zero, zero, not last)
        u_lat, y_lat, _, _ = mixer(x, m_lat, s_fwd, s_bwd, True)
        p_lat = _pool2d_call(u_lat, p["pool_w"], p["pool_scale"], cfg["tp_pool"])
        x = residuals(x, m_lat, y_lat, p_lat, last)
        if not last:
            p_ctx = _pool1d_call(u_ctx, p["pool_w"], p["pool_scale"])
            ctx_h = residuals(ctx_h, m_ctx, y_ctx, p_ctx, False)
    return x


def kernel(x, c, ctx, c_ctx, ada_w, ada_b, norm_mix, norm_ffn, w_in, conv_rkv, decay_bias, decay_up,
           iclr_bias, iclr_up, gate_up, k_k, k_a, r_k, gn_w, gn_b, pool_w, pool_scale, w_out,
           ffn_gate, ffn_up, ffn_down, final_norm):
    return _forward(_CFG, x, c, ctx, c_ctx, ada_w, ada_b, norm_mix, norm_ffn, w_in, conv_rkv, decay_bias,
                    decay_up, iclr_bias, iclr_up, gate_up, k_k, k_a, r_k, gn_w, gn_b, pool_w, pool_scale,
                    w_out, ffn_gate, ffn_up, ffn_down, final_norm)
```

```python
import functools

import numpy as np
import jax
import jax.numpy as jnp
from jax import lax
from jax.experimental import pallas as pl
from jax.experimental.pallas import tpu as pltpu

F32 = jnp.float32
BF16 = jnp.bfloat16

HEAD = 64
LANES = 128
GRID_COLS = 64
POOL_WINS = (2, 4, 8, 16)
POOL_HALO_ROWS = 8
RMS_EPS = 1e-6
GN_EPS = 64e-5
DECAY_LORA = 96
ICLR_LORA = 96
GATE_LORA = 256
LORA_PAD = 128
VMEM_LIMIT = 56 * 1024 * 1024


def _cparams(sem):
    return pltpu.CompilerParams(dimension_semantics=sem, vmem_limit_bytes=VMEM_LIMIT)


def _dot(a, b):
    return jnp.dot(a, b, preferred_element_type=F32)


def _dot_nt(a, b):
    return lax.dot_general(a, b, (((1,), (1,)), ((), ())), preferred_element_type=F32)


def _dot_tn(a, b):
    return lax.dot_general(a, b, (((0,), (0,)), ((), ())), preferred_element_type=F32)


def _split(x):
    hi = x.astype(BF16)
    lo = (x - hi.astype(F32)).astype(BF16)
    return hi, lo


def _head_sum(x, ones_bd):
    hi, lo = _split(x)
    parts = []
    for c in range(x.shape[1] // LANES):
        sl = slice(c * LANES, (c + 1) * LANES)
        parts.append(_dot(hi[:, sl], ones_bd) + _dot(lo[:, sl], ones_bd))
    return parts[0] if len(parts) == 1 else jnp.concatenate(parts, axis=1)


def _sigmoid(x):
    return 1.0 / (1.0 + jnp.exp(-x))


def _silu(x):
    return x * _sigmoid(x)


def _ada_kernel(c_ref, w_ref, b_ref, o_ref):
    c = c_ref[...]
    o_ref[0] = _dot(_silu(c).astype(BF16), w_ref[0].astype(BF16)) + b_ref[0]


def _ada_call(cond, ada_w, ada_b, tn):
    L, D, N = ada_w.shape
    R = cond.shape[0]
    return pl.pallas_call(
        _ada_kernel,
        grid=(L, N // tn),
        in_specs=[
            pl.BlockSpec((R, D), lambda l, j: (0, 0)),
            pl.BlockSpec((1, D, tn), lambda l, j: (l, 0, j)),
            pl.BlockSpec((1, 1, tn), lambda l, j: (l, 0, j)),
        ],
        out_specs=pl.BlockSpec((1, R, tn), lambda l, j: (l, 0, j)),
        out_shape=jax.ShapeDtypeStruct((L, R, N), F32),
        compiler_params=_cparams(("parallel", "parallel")),
        name="ada_modulation",
    )(cond, ada_w, ada_b.reshape(L, 1, N))


def _modnorm(x, g, shift, scale):
    ms = jnp.mean(x * x, axis=-1, keepdims=True)
    return (x * lax.rsqrt(ms + RMS_EPS) * g) * (1.0 + scale) + shift


HALO = 8


def _rkv_kernel(x_ref, xp_ref, xn_ref, g_ref, sh_ref, sc_ref, w_ref, cw, kkw, ones_ref,
                r_o, k_o, v_o, kk_o):
    i = pl.program_id(1)
    has_prev = (i > 0).astype(F32)
    has_next = (i < pl.num_programs(1) - 1).astype(F32)
    tm = x_ref.shape[1]
    W = r_o.shape[2]
    norm = lambda x: _modnorm(x, g_ref[...], sh_ref[0], sc_ref[0])
    xn = jnp.concatenate([norm(xp_ref[0]) * has_prev, norm(x_ref[0]), norm(xn_ref[0]) * has_next], axis=0)
    xn = xn.astype(BF16)
    rows = slice(HALO, HALO + tm)

    def conv_proj(c0):
        u = _dot(xn, w_ref[:, c0:c0 + W])
        w = cw[:, c0:c0 + W]
        return (pltpu.roll(u, 1, 0)[rows] * w[0:1, :] + u[rows] * w[1:2, :]
                + pltpu.roll(u, tm + 2 * HALO - 1, 0)[rows] * w[2:3, :])

    r_o[0] = conv_proj(0).astype(r_o.dtype)
    k = conv_proj(W)
    k_o[0] = k.astype(k_o.dtype)
    v_o[0] = conv_proj(2 * W).astype(v_o.dtype)
    kk = k * kkw[...]
    kk = kk * lax.rsqrt(jnp.maximum(_head_sum(kk * kk, ones_ref[...]), 1e-24))
    kk_o[0] = kk.astype(kk_o.dtype)


def _rkv_call(x, g, shift, scale, p, tm):
    B, T, D = x.shape
    W = p["k_k"].shape[1]
    tm = min(tm, T)
    nb = T // HALO
    halo = lambda f: pl.BlockSpec((1, HALO, D), lambda b, i: (b, f(i), 0))
    full = lambda a: pl.BlockSpec(a.shape, lambda b, i: (0,) * a.ndim)
    mod = pl.BlockSpec((1, 1, D), lambda b, i: (b, 0, 0))
    out_blk = pl.BlockSpec((1, tm, W), lambda b, i: (b, i, 0))
    return pl.pallas_call(
        _rkv_kernel,
        grid=(B, T // tm),
        in_specs=[pl.BlockSpec((1, tm, D), lambda b, i: (b, i, 0)),
                  halo(lambda i: jnp.maximum(i * (tm // HALO) - 1, 0)),
                  halo(lambda i: jnp.minimum((i + 1) * (tm // HALO), nb - 1)),
                  pl.BlockSpec((1, D), lambda b, i: (0, 0)), mod, mod,
                  pl.BlockSpec((D, 3 * W), lambda b, i: (0, 0), pipeline_mode=pl.Buffered(1)),
                  full(p["conv"]), full(p["k_k"]), full(p["ones_bd"])],
        out_specs=[out_blk] * 4,
        out_shape=[jax.ShapeDtypeStruct((B, T, W), BF16)] * 4,
        compiler_params=_cparams(("parallel", "parallel")),
        name="norm_inproj_rkv",
    )(x, x, x, g, shift, scale, p["w_in"], p["conv"], p["k_k"], p["ones_bd"])


def _lora_kernel(x_ref, g_ref, sh_ref, sc_ref, w_ref, r_ref, k_ref, v_ref, dbias, dup, ibias, iup, gup,
                 ka, rk, ones_ref, up_o, a0_o, a1_o, lw0_o, lw1_o, g_o, bonus_o):
    PW = up_o.shape[2]
    xn = _modnorm(x_ref[0], g_ref[...], sh_ref[0], sc_ref[0]).astype(BF16)
    lo = _dot(xn, w_ref[:, PW:])
    w_lo = jnp.tanh(lo[:, 0:LORA_PAD]).astype(BF16)
    a_lo = lo[:, LORA_PAD:2 * LORA_PAD].astype(BF16)
    g_lo = _sigmoid(lo[:, 2 * LORA_PAD:2 * LORA_PAD + GATE_LORA]).astype(BF16)
    up_o[0] = _dot(xn, w_ref[:, 0:PW])
    a_sum = None
    for d, (a_o, lw_o) in enumerate(((a0_o, lw0_o), (a1_o, lw1_o))):
        z = dbias[d:d + 1, :] + _dot(w_lo, dup[d])
        w_log = jnp.minimum(z, 0.0) - jnp.log(1.0 + jnp.exp(-jnp.abs(z))) - 0.5
        lw_o[0] = -jnp.exp(w_log)
        a = _sigmoid(ibias[d:d + 1, :] + _dot(a_lo, iup[d]))
        a_o[0] = a.astype(a_o.dtype)
        a_sum = a if a_sum is None else a_sum + a
    g_o[0] = _dot(g_lo, gup[...]).astype(g_o.dtype)
    r, k, v = (z[0].astype(F32) for z in (r_ref, k_ref, v_ref))
    kpair = k * (2.0 + (a_sum - 2.0) * ka[...])
    bonus_o[0] = (_head_sum(r * kpair * rk[...], ones_ref[...]) * v).astype(bonus_o.dtype)


def _lora_call(x, g, shift, scale, p, r, k, v, tm):
    B, T, D = x.shape
    W = p["k_k"].shape[1]
    PW = p["pool_scale"].shape[1]
    NL = p["w_in"].shape[1] - 3 * W
    tm = min(tm, T)
    full = lambda a: pl.BlockSpec(a.shape, lambda b, i: (0,) * a.ndim)
    mod = pl.BlockSpec((1, 1, D), lambda b, i: (b, 0, 0))
    consts = [p["dbias"], p["dup"], p["ibias"], p["iup"], p["gup"], p["k_a"], p["r_k"], p["ones_bd"]]
    blk = lambda n: pl.BlockSpec((1, tm, n), lambda b, i: (b, i, 0))
    sds = lambda n, dt: jax.ShapeDtypeStruct((B, T, n), dt)
    return pl.pallas_call(
        _lora_kernel,
        grid=(B, T // tm),
        in_specs=[blk(D), pl.BlockSpec((1, D), lambda b, i: (0, 0)), mod, mod,
                  pl.BlockSpec((D, NL), lambda b, i: (0, 3 * W // NL), pipeline_mode=pl.Buffered(1)),
                  blk(W), blk(W), blk(W)] + [full(a) for a in consts],
        out_specs=[blk(PW)] + [blk(W)] * 6,
        out_shape=[sds(PW, F32), sds(W, BF16), sds(W, BF16), sds(W, F32), sds(W, F32), sds(W, BF16),
                   sds(W, BF16)],
        compiler_params=_cparams(("parallel", "parallel")),
        name="norm_inproj_lora_pool",
    )(x, g, shift, scale, p["w_in"], r, k, v, *consts)


def _scan_masks(C):
    t = np.arange(C)[:, None]
    j = np.arange(2 * C)[None, :] % C
    fwd = [t > j, t >= j]
    rev = [t < j, t <= j]
    s = 1
    while s < C:
        blk = (t // (2 * s)) == (j // (2 * s))
        fwd.append(blk & ((t % (2 * s)) >= s) & ((j % (2 * s)) < s))
        rev.append(blk & ((t % (2 * s)) < s) & ((j % (2 * s)) >= s))
        s *= 2
    return np.stack(fwd + rev).astype(np.float32)


def _chunk_group(chains, masks, nlev, m0, m1, eye):
    n = len(chains)
    C = chains[0][0].shape[0]
    P = 2 * C
    stack = lambda xb: jnp.concatenate([xb * m0, xb * m1], axis=0)
    AR, BK, Vs = [], [], []
    for (At, Rt, Bt, Kt, v, _, _, _) in chains:
        AR.append(jnp.concatenate([At, Rt], axis=0).astype(BF16))
        BK.append(jnp.concatenate([stack(Bt.astype(BF16)), stack(Kt.astype(BF16))], axis=0))
        Vs.append(stack(v.astype(BF16)))
    sc = [_dot_nt(AR[i], BK[i]).astype(BF16) for i in range(n)]
    x1 = [_dot_nt(AR[i], chains[i][6].astype(BF16)) for i in range(n)]
    Nab = [sc[i][:C, :P] * masks[chains[i][7]] for i in range(n)]
    T = [eye + (Nab[i] * masks[chains[i][7] + 2]).astype(F32) for i in range(n)]
    for lv in range(1, nlev):
        Tb = [T[i].astype(BF16) for i in range(n)]
        Nl = [stack(Nab[i] * masks[chains[i][7] + 2 + lv]) for i in range(n)]
        TN = [_dot(Tb[i], Nl[i]).astype(BF16) for i in range(n)]
        T = [T[i] + _dot(TN[i], stack(Tb[i])) for i in range(n)]
    NV = [_dot(sc[i][:C, P:] * masks[chains[i][7]], Vs[i]) for i in range(n)]
    Us = [_dot(T[i].astype(BF16), stack((x1[i][:C] + NV[i]).astype(BF16))) for i in range(n)]
    UV = [jnp.concatenate([stack(Us[i].astype(BF16)), Vs[i]], axis=0) for i in range(n)]
    out = []
    for i in range(n):
        incl = masks[chains[i][7] + 1]
        Pr = jnp.concatenate([sc[i][C:, :P] * incl, sc[i][C:, P:] * incl], axis=1)
        y = x1[i][C:] + _dot(Pr, UV[i])
        Zn = chains[i][5] * (chains[i][6] + _dot_tn(UV[i], BK[i]))
        out.append((y, Zn))
    return out


def _scan_kernel(rf, kf, vf, kkf, af, lwf, rb, kb, vb, kkb, ab, lwb, s0f, s0b, ka_ref, tri_ref, m_ref,
                 yf_o, yb_o, sf_o, sb_o, *, C, G, nchunk):
    @pl.when(pl.program_id(2) == 0)
    def _():
        sf_o[...] = s0f[...]
        sb_o[...] = s0b[...]

    nlev = int(np.log2(C))
    nmask = 2 + nlev
    lane = lax.broadcasted_iota(jnp.int32, (1, LANES), 1)
    m0 = (lane < HEAD).astype(BF16)
    m1 = (lane >= HEAD).astype(BF16)
    ri = lax.broadcasted_iota(jnp.int32, (C, 2 * C), 0)
    ci = lax.broadcasted_iota(jnp.int32, (C, 2 * C), 1)
    eye = (ri == jnp.bitwise_and(ci, C - 1)).astype(F32)
    ka = ka_ref[...]
    dirs = ((rf, kf, vf, kkf, af, lwf, yf_o, sf_o, False), (rb, kb, vb, kkb, ab, lwb, yb_o, sb_o, True))
    for cidx in range(nchunk):
        chains, dests = [], []
        for (r_r, k_r, v_r, kk_r, a_r, lw_r, y_o, s_o, rev) in dirs:
            c = (nchunk - 1 - cidx) if rev else cidx
            rows = slice(c * C, (c + 1) * C)
            r, k, kk, a = (z[0, rows, :].astype(F32) for z in (r_r, k_r, kk_r, a_r))
            v, lw = v_r[0, rows, :], lw_r[0, rows, :]
            hi, lo = _split(lw)
            tri = tri_ref[1 if rev else 0]
            cum = _dot(tri, hi) + _dot(tri, lo)
            ea = jnp.exp(cum - lw)
            er = jnp.exp(cum)
            einv = jnp.exp(-cum)
            At = -(kk * ea)
            Rt = r * er
            Bt = (kk * a) * einv
            Kt = (k * (1.0 + (a - 1.0) * ka)) * einv
            gam = er[0:1, :] if rev else er[C - 1:C, :]
            for g in range(G):
                sl = slice(g * LANES, (g + 1) * LANES)
                chains.append((At[:, sl], Rt[:, sl], Bt[:, sl], Kt[:, sl], v[:, sl], gam[:, sl],
                               s_o[0, g], nmask if rev else 0))
            dests.append((y_o, s_o, rows))
        res = _chunk_group(chains, m_ref, nlev, m0, m1, eye)
        for d, (y_o, s_o, rows) in enumerate(dests):
            part = res[d * G:(d + 1) * G]
            for g in range(G):
                s_o[0, g] = part[g][1]
            y = part[0][0] if G == 1 else jnp.concatenate([q[0] for q in part], axis=1)
            y_o[0, rows, :] = y.astype(y_o.dtype)


def _scan_call(pre, a0, a1, lw0, lw1, s0f, s0b, ka, C, G, TB):
    r, k, v, kk = pre
    B, T, W = r.shape
    TB = min(TB, T)
    NT = T // TB
    NG = W // (G * LANES)
    GW = G * LANES
    fwd = pl.BlockSpec((1, TB, GW), lambda b, g, i: (b, i, g))
    bwd = pl.BlockSpec((1, TB, GW), lambda b, g, i: (b, NT - 1 - i, g))
    st = pl.BlockSpec((1, G, LANES, LANES), lambda b, g, i: (b, g, 0, 0))
    masks = jnp.asarray(_scan_masks(C), BF16)
    tri = jnp.asarray(np.stack([np.tril(np.ones((C, C), np.float32)),
                                np.triu(np.ones((C, C), np.float32))])).astype(BF16)
    kern = functools.partial(_scan_kernel, C=C, G=G, nchunk=TB // C)
    return pl.pallas_call(
        kern,
        grid=(B, NG, NT),
        in_specs=[fwd] * 6 + [bwd] * 6 + [st, st,
                  pl.BlockSpec((1, GW), lambda b, g, i: (0, g)),
                  pl.BlockSpec(tri.shape, lambda b, g, i: (0, 0, 0)),
                  pl.BlockSpec(masks.shape, lambda b, g, i: (0, 0, 0))],
        out_specs=[fwd, bwd, st, st],
        out_shape=[jax.ShapeDtypeStruct((B, T, W), BF16)] * 2
                  + [jax.ShapeDtypeStruct(s0f.shape, F32)] * 2,
        compiler_params=_cparams(("parallel", "parallel", "arbitrary")),
        name="wkv7_scan",
    )(r, k, v, kk, a0, lw0, r, k, v, kk, a1, lw1, s0f, s0b, ka, tri, masks)


def _window_count(pos, half, n):
    return (jnp.minimum(pos + half, n) - jnp.maximum(pos - half, 0)).astype(F32)


def _pool2d_kernel(cur, prv, nxt, pw_ref, ps_ref, o_ref, col_ref, *, tp, halo, nrows):
    i = pl.program_id(1)
    has_prev = (i > 0).astype(F32)
    has_next = (i < pl.num_programs(1) - 1).astype(F32)
    PG = pw_ref.shape[1]
    shift = GRID_COLS.bit_length() - 1
    tok = lax.broadcasted_iota(jnp.int32, (tp, PG), 0)
    col = jnp.bitwise_and(tok, GRID_COLS - 1)
    row = i * (tp // GRID_COLS) + jnp.right_shift(tok, shift)
    p = lax.broadcasted_iota(jnp.int32, (LANES, LANES), 0)
    q = lax.broadcasted_iota(jnp.int32, (LANES, LANES), 1)
    same_row = jnp.right_shift(p, shift) == jnp.right_shift(q, shift)
    dcol = jnp.bitwise_and(p, GRID_COLS - 1) - jnp.bitwise_and(q, GRID_COLS - 1)
    for gi, win in enumerate(POOL_WINS):
        half = win // 2
        cs = slice(gi * PG, (gi + 1) * PG)
        band = (same_row & (dcol <= half) & (dcol > -half)).astype(BF16)

        def colsum(x):
            hi, lo = _split(x)
            return _dot(band, hi) + _dot(band, lo)

        for c in range(halo // LANES):
            rs = slice(c * LANES, (c + 1) * LANES)
            col_ref[rs, :] = colsum(prv[0, rs, cs] * has_prev)
            rs2 = slice(halo + tp + c * LANES, halo + tp + (c + 1) * LANES)
            col_ref[rs2, :] = colsum(nxt[0, rs, cs] * has_next)
        for c in range(tp // LANES):
            rs = slice(c * LANES, (c + 1) * LANES)
            col_ref[halo + c * LANES:halo + (c + 1) * LANES, :] = colsum(cur[0, rs, cs])
        acc = None
        for dr in range(-half, half):
            part = col_ref[halo + dr * GRID_COLS:halo + dr * GRID_COLS + tp, :]
            acc = part if acc is None else acc + part
        cnt = _window_count(col, half, GRID_COLS) * _window_count(row, half, nrows)
        diff = acc * (1.0 / cnt) - cur[0, :, cs]
        o_ref[0, :, cs] = (_dot(diff.astype(BF16), pw_ref[gi]) * ps_ref[:, cs]).astype(o_ref.dtype)


def _pool2d_call(u, pool_w, pool_scale, tp):
    B, T, _ = u.shape
    PW = pool_scale.shape[1]
    halo = POOL_HALO_ROWS * GRID_COLS
    tp = min(tp, T)
    nh = T // halo
    kern = functools.partial(_pool2d_kernel, tp=tp, halo=halo, nrows=T // GRID_COLS)
    return pl.pallas_call(
        kern,
        grid=(B, T // tp),
        in_specs=[
            pl.BlockSpec((1, tp, PW), lambda b, i: (b, i, 0)),
            pl.BlockSpec((1, halo, PW), lambda b, i: (b, jnp.maximum(i * (tp // halo) - 1, 0), 0)),
            pl.BlockSpec((1, halo, PW), lambda b, i: (b, jnp.minimum((i + 1) * (tp // halo), nh - 1), 0)),
            pl.BlockSpec(pool_w.shape, lambda b, i: (0, 0, 0)),
            pl.BlockSpec((1, PW), lambda b, i: (0, 0)),
        ],
        out_specs=pl.BlockSpec((1, tp, PW), lambda b, i: (b, i, 0)),
        out_shape=jax.ShapeDtypeStruct((B, T, PW), BF16),
        scratch_shapes=[pltpu.VMEM((tp + 2 * halo, pool_w.shape[1]), F32)],
        compiler_params=_cparams(("parallel", "parallel")),
        name="pool2d",
    )(u, u, u, pool_w, pool_scale)


def _pool1d_kernel(u_ref, pw_ref, ps_ref, o_ref):
    T = u_ref.shape[1]
    PG = pw_ref.shape[1]
    p = lax.broadcasted_iota(jnp.int32, (T, T), 0)
    q = lax.broadcasted_iota(jnp.int32, (T, T), 1)
    pos = lax.broadcasted_iota(jnp.int32, (T, PG), 0)
    for gi, win in enumerate(POOL_WINS):
        half = win // 2
        cs = slice(gi * PG, (gi + 1) * PG)
        band = ((p - q <= half) & (p - q > -half)).astype(BF16)
        x = u_ref[0, :, cs]
        hi, lo = _split(x)
        m = (_dot(band, hi) + _dot(band, lo)) * (1.0 / _window_count(pos, half, T))
        o_ref[0, :, cs] = (_dot((m - x).astype(BF16), pw_ref[gi]) * ps_ref[:, cs]).astype(o_ref.dtype)


def _pool1d_call(u, pool_w, pool_scale):
    B, T, _ = u.shape
    PW = pool_scale.shape[1]
    return pl.pallas_call(
        _pool1d_kernel,
        grid=(B,),
        in_specs=[
            pl.BlockSpec((1, T, PW), lambda b: (b, 0, 0)),
            pl.BlockSpec(pool_w.shape, lambda b: (0, 0, 0)),
            pl.BlockSpec((1, PW), lambda b: (0, 0)),
        ],
        out_specs=pl.BlockSpec((1, T, PW), lambda b: (b, 0, 0)),
        out_shape=jax.ShapeDtypeStruct((B, T, PW), BF16),
        compiler_params=_cparams(("parallel",)),
        name="pool1d",
    )(u, pool_w, pool_scale)


def _outproj_kernel(x_ref, yf, yb, bonus, g, p_ref, gnw, gnb, ones_ref, gt_ref, w_ref, o_ref):
    W = yf.shape[2]
    ones_bd = ones_ref[...]
    y = yf[0].astype(F32) + yb[0].astype(F32)
    mu = _head_sum(y, ones_bd) * (1.0 / HEAD)
    yc = y - mu
    var = _head_sum(yc * yc, ones_bd) * (1.0 / HEAD)
    yn = yc * lax.rsqrt(var + GN_EPS) * gnw[...] + gnb[...]
    mixed = ((yn + bonus[0].astype(F32)) * g[0].astype(F32)).astype(BF16)
    acc = _dot(mixed, w_ref[0:W, :]) + _dot(p_ref[0].astype(BF16), w_ref[W:, :])
    o_ref[0] = x_ref[0] + gt_ref[0] * acc


def _outproj_call(x, yf, yb, bonus, g, p, gnw, gnb, ones_bd, gate, w, tm):
    B, T, D = x.shape
    W, PW = yf.shape[2], p.shape[2]
    tm = min(tm, T)
    blk = lambda n: pl.BlockSpec((1, tm, n), lambda b, i: (b, i, 0))
    vec = pl.BlockSpec((1, W), lambda b, i: (0, 0))
    return pl.pallas_call(
        _outproj_kernel,
        grid=(B, T // tm),
        in_specs=[blk(D), blk(W), blk(W), blk(W), blk(W), blk(PW), vec, vec,
                  pl.BlockSpec(ones_bd.shape, lambda b, i: (0, 0)),
                  pl.BlockSpec((1, 1, D), lambda b, i: (b, 0, 0)),
                  pl.BlockSpec(w.shape, lambda b, i: (0, 0), pipeline_mode=pl.Buffered(1))],
        out_specs=blk(D),
        out_shape=jax.ShapeDtypeStruct((B, T, D), F32),
        compiler_params=_cparams(("parallel", "parallel")),
        name="mix_outproj_residual",
    )(x, yf, yb, bonus, g, p, gnw, gnb, ones_bd, gate, w)


def _ffn_kernel(x_ref, g_ref, sh_ref, sc_ref, gt_ref, wg_ref, wu_ref, wd_ref, fin_ref, o_ref, xn_ref, acc_ref,
                *, final):
    j = pl.program_id(2)

    @pl.when(j == 0)
    def _():
        xn_ref[...] = _modnorm(x_ref[0], g_ref[...], sh_ref[0], sc_ref[0]).astype(BF16)
        acc_ref[...] = jnp.zeros_like(acc_ref)

    xn = xn_ref[...]
    h = _silu(_dot(xn, wg_ref[...])) * _dot(xn, wu_ref[...])
    acc_ref[...] += _dot(h.astype(BF16), wd_ref[...])

    @pl.when(j == pl.num_programs(2) - 1)
    def _():
        h = x_ref[0] + gt_ref[0] * acc_ref[...]
        if final:
            h = h * lax.rsqrt(jnp.mean(h * h, axis=-1, keepdims=True) + RMS_EPS) * fin_ref[...]
        o_ref[0] = h


def _ffn_call(x, g, shift, scale, gate, wg, wu, wd, fin, final, tm, tf):
    B, T, D = x.shape
    F = wg.shape[1]
    tm = min(tm, T)
    mod = pl.BlockSpec((1, 1, D), lambda b, i, j: (b, 0, 0))
    return pl.pallas_call(
        functools.partial(_ffn_kernel, final=final),
        grid=(B, T // tm, F // tf),
        in_specs=[
            pl.BlockSpec((1, tm, D), lambda b, i, j: (b, i, 0)),
            pl.BlockSpec((1, D), lambda b, i, j: (0, 0)),
            mod, mod, mod,
            pl.BlockSpec((D, tf), lambda b, i, j: (0, j)),
            pl.BlockSpec((D, tf), lambda b, i, j: (0, j)),
            pl.BlockSpec((tf, D), lambda b, i, j: (j, 0)),
            pl.BlockSpec((1, D), lambda b, i, j: (0, 0)),
        ],
        out_specs=pl.BlockSpec((1, tm, D), lambda b, i, j: (b, i, 0)),
        out_shape=jax.ShapeDtypeStruct((B, T, D), F32),
        scratch_shapes=[pltpu.VMEM((tm, D), BF16), pltpu.VMEM((tm, D), F32)],
        compiler_params=_cparams(("parallel", "parallel", "arbitrary")),
        name="norm_swiglu_residual",
    )(x, g, shift, scale, gate, wg, wu, wd, fin)


def _pad_rows(w, n):
    return jnp.pad(w, ((0, 0),) * (w.ndim - 2) + ((0, n - w.shape[-2]), (0, 0)))


def _layer_params(l, w_in, conv_rkv, decay_bias, decay_up, iclr_bias, iclr_up, gate_up, k_k, k_a, r_k,
                  gn_w, gn_b, pool_w, pool_scale, w_out, ffn_gate, ffn_up, ffn_down):
    RW = k_k.shape[1]
    wl = w_in[l]
    o_w = 3 * RW
    o_a = o_w + DECAY_LORA
    o_g = o_a + ICLR_LORA
    o_p = o_g + GATE_LORA
    padc = lambda w: jnp.pad(w, ((0, 0), (0, LORA_PAD - w.shape[1])))
    w_in_r = jnp.concatenate(
        [wl[:, :o_w], wl[:, o_p:], padc(wl[:, o_w:o_a]), padc(wl[:, o_a:o_g]), wl[:, o_g:o_p]], axis=1)
    row = lambda a: a[l].reshape(1, -1)
    return {
        "w_in": w_in_r.astype(BF16),
        "conv": conv_rkv[l],
        "dbias": decay_bias[l], "dup": _pad_rows(decay_up[l], LORA_PAD).astype(BF16),
        "ibias": iclr_bias[l], "iup": _pad_rows(iclr_up[l], LORA_PAD).astype(BF16),
        "gup": gate_up[l].astype(BF16),
        "k_k": row(k_k), "k_a": row(k_a), "r_k": row(r_k), "gn_w": row(gn_w), "gn_b": row(gn_b),
        "pool_w": pool_w[l].astype(BF16), "pool_scale": row(pool_scale),
        "w_out": w_out[l].astype(BF16),
        "wg": ffn_gate[l].astype(BF16), "wu": ffn_up[l].astype(BF16), "wd": ffn_down[l].astype(BF16),
    }


_CFG = dict(tm_in=512, scan_c=64, scan_g=8, scan_tb=128, tp_pool=2048, tm_out=512,
            tm_ffn=512, tf_ffn=512, tn_ada=1024)


def _forward(cfg, x, c, ctx, c_ctx, ada_w, ada_b, norm_mix, norm_ffn, w_in, conv_rkv, decay_bias, decay_up,
             iclr_bias, iclr_up, gate_up, k_k, k_a, r_k, gn_w, gn_b, pool_w, pool_scale, w_out,
             ffn_gate, ffn_up, ffn_down, final_norm):
    B, T, D = x.shape
    L = ada_w.shape[0]
    RW = k_k.shape[1]
    npair = RW // LANES
    ones_bd = jnp.asarray(np.kron(np.eye(LANES // HEAD), np.ones((HEAD, HEAD))), BF16)
    fin = final_norm.reshape(1, D)

    cond = jnp.zeros((8, D), F32).at[:B].set(c).at[B].set(c_ctx)
    mods = _ada_call(cond, ada_w, ada_b, cfg["tn_ada"])
    mods = mods.reshape(L, 8, 6, D)

    ctx_h = ctx
    for l in range(L):
        last = l == L - 1
        p = _layer_params(l, w_in, conv_rkv, decay_bias, decay_up, iclr_bias, iclr_up, gate_up, k_k, k_a,
                          r_k, gn_w, gn_b, pool_w, pool_scale, w_out, ffn_gate, ffn_up, ffn_down)
        p["ones_bd"] = ones_bd
        m_lat = [mods[l, :B, s].reshape(B, 1, D) for s in range(6)]
        m_ctx = [jnp.broadcast_to(mods[l, B, s].reshape(1, 1, D), (B, 1, D)) for s in range(6)]
        g_mix = norm_mix[l].reshape(1, D)
        g_ffn = norm_ffn[l].reshape(1, D)
        zero = jnp.zeros((B, npair, LANES, LANES), F32)

        def mixer(h, m, s0f, s0b, need_out):
            r, k, v, kk = _rkv_call(h, g_mix, m[0], m[1], p, cfg["tm_in"])
            u, a0, a1, lw0, lw1, g, bonus = _lora_call(h, g_mix, m[0], m[1], p, r, k, v, cfg["tm_in"])
            yf, yb, sf, sb = _scan_call((r, k, v, kk), a0, a1, lw0, lw1, s0f, s0b, p["k_a"],
                                        cfg["scan_c"], cfg["scan_g"], cfg["scan_tb"])
            return u, (yf, yb, bonus, g), sf, sb

        def residuals(h, m, y, pooled, final):
            h = _outproj_call(h, *y, pooled, p["gn_w"], p["gn_b"], ones_bd, m[2], p["w_out"], cfg["tm_out"])
            return _ffn_call(h, g_ffn, m[3], m[4], m[5], p["wg"], p["wu"], p["wd"], fin, final,
                             cfg["tm_ffn"], cfg["tf_ffn"])

        u_ctx, y_ctx, s_fwd, s_bwd = mixer(ctx_h, m_ctx, zero, zero, not last)
        u_lat, y_lat, _, _ = mixer(x, m_lat, s_fwd, s_bwd, True)
        p_lat = _pool2d_call(u_lat, p["pool_w"], p["pool_scale"], cfg["tp_pool"])
        x = residuals(x, m_lat, y_lat, p_lat, last)
        if not last:
            p_ctx = _pool1d_call(u_ctx, p["pool_w"], p["pool_scale"])
            ctx_h = residuals(ctx_h, m_ctx, y_ctx, p_ctx, False)
    return x


def kernel(x, c, ctx, c_ctx, ada_w, ada_b, norm_mix, norm_ffn, w_in, conv_rkv, decay_bias, decay_up,
           iclr_bias, iclr_up, gate_up, k_k, k_a, r_k, gn_w, gn_b, pool_w, pool_scale, w_out,
           ffn_gate, ffn_up, ffn_down, final_norm):
    return _forward(_CFG, x, c, ctx, c_ctx, ada_w, ada_b, norm_mix, norm_ffn, w_in, conv_rkv, decay_bias,
                    decay_up, iclr_bias, iclr_up, gate_up, k_k, k_a, r_k, gn_w, gn_b, pool_w, pool_scale,
                    w_out, ffn_gate, ffn_up, ffn_down, final_norm)
```

```python
import functools

import numpy as np
import jax
import jax.numpy as jnp
from jax import lax
from jax.experimental import pallas as pl
from jax.experimental.pallas import tpu as pltpu

F32 = jnp.float32
BF16 = jnp.bfloat16

HEAD = 64
LANES = 128
GRID_COLS = 64
POOL_WINS = (2, 4, 8, 16)
POOL_HALO_ROWS = 8
RMS_EPS = 1e-6
GN_EPS = 64e-5
DECAY_LORA = 96
ICLR_LORA = 96
GATE_LORA = 256
LORA_PAD = 128
VMEM_LIMIT = 56 * 1024 * 1024


def _cparams(sem):
    return pltpu.CompilerParams(dimension_semantics=sem, vmem_limit_bytes=VMEM_LIMIT)


def _dot(a, b):
    return jnp.dot(a, b, preferred_element_type=F32)


def _dot_nt(a, b):
    return lax.dot_general(a, b, (((1,), (1,)), ((), ())), preferred_element_type=F32)


def _dot_tn(a, b):
    return lax.dot_general(a, b, (((0,), (0,)), ((), ())), preferred_element_type=F32)


def _split(x):
    hi = x.astype(BF16)
    lo = (x - hi.astype(F32)).astype(BF16)
    return hi, lo


def _head_sum(x, ones_bd):
    xb = x.astype(BF16)
    parts = [_dot(xb[:, c * LANES:(c + 1) * LANES], ones_bd) for c in range(x.shape[1] // LANES)]
    return parts[0] if len(parts) == 1 else jnp.concatenate(parts, axis=1)


def _sigmoid(x):
    return 1.0 / (1.0 + jnp.exp(-x))


def _silu(x):
    return x * _sigmoid(x)


def _ada_kernel(c_ref, w_ref, b_ref, o_ref):
    c = c_ref[...]
    o_ref[0] = _dot(_silu(c).astype(BF16), w_ref[0].astype(BF16)) + b_ref[0]


def _ada_call(cond, ada_w, ada_b, tn):
    L, D, N = ada_w.shape
    R = cond.shape[0]
    return pl.pallas_call(
        _ada_kernel,
        grid=(L, N // tn),
        in_specs=[
            pl.BlockSpec((R, D), lambda l, j: (0, 0)),
            pl.BlockSpec((1, D, tn), lambda l, j: (l, 0, j)),
            pl.BlockSpec((1, 1, tn), lambda l, j: (l, 0, j)),
        ],
        out_specs=pl.BlockSpec((1, R, tn), lambda l, j: (l, 0, j)),
        out_shape=jax.ShapeDtypeStruct((L, R, N), F32),
        compiler_params=_cparams(("parallel", "parallel")),
        name="ada_modulation",
    )(cond, ada_w, ada_b.reshape(L, 1, N))


def _modnorm(x, g, shift, scale):
    ms = jnp.mean(x * x, axis=-1, keepdims=True)
    return (x * lax.rsqrt(ms + RMS_EPS) * g) * (1.0 + scale) + shift


HALO = 8


def _rkv_kernel(x_ref, xp_ref, xn_ref, g_ref, sh_ref, sc_ref, w_ref, cw, kkw, ones_ref,
                r_o, k_o, v_o, kk_o):
    i = pl.program_id(1)
    has_prev = (i > 0).astype(F32)
    has_next = (i < pl.num_programs(1) - 1).astype(F32)
    tm = x_ref.shape[1]
    W = r_o.shape[2]
    norm = lambda x: _modnorm(x, g_ref[...], sh_ref[0], sc_ref[0])
    xn = jnp.concatenate([norm(xp_ref[0]) * has_prev, norm(x_ref[0]), norm(xn_ref[0]) * has_next], axis=0)
    xn = xn.astype(BF16)
    rows = slice(HALO, HALO + tm)

    def conv_proj(c0):
        u = _dot(xn, w_ref[:, c0:c0 + W])
        w = cw[:, c0:c0 + W]
        return (pltpu.roll(u, 1, 0)[rows] * w[0:1, :] + u[rows] * w[1:2, :]
                + pltpu.roll(u, tm + 2 * HALO - 1, 0)[rows] * w[2:3, :])

    r_o[0] = conv_proj(0).astype(r_o.dtype)
    k = conv_proj(W)
    k_o[0] = k.astype(k_o.dtype)
    v_o[0] = conv_proj(2 * W).astype(v_o.dtype)
    kk = k * kkw[...]
    kk = kk * lax.rsqrt(jnp.maximum(_head_sum(kk * kk, ones_ref[...]), 1e-24))
    kk_o[0] = kk.astype(kk_o.dtype)


def _rkv_call(x, g, shift, scale, p, tm):
    B, T, D = x.shape
    W = p["k_k"].shape[1]
    tm = min(tm, T)
    nb = T // HALO
    halo = lambda f: pl.BlockSpec((1, HALO, D), lambda b, i: (b, f(i), 0))
    full = lambda a: pl.BlockSpec(a.shape, lambda b, i: (0,) * a.ndim)
    mod = pl.BlockSpec((1, 1, D), lambda b, i: (b, 0, 0))
    out_blk = pl.BlockSpec((1, tm, W), lambda b, i: (b, i, 0))
    return pl.pallas_call(
        _rkv_kernel,
        grid=(B, T // tm),
        in_specs=[pl.BlockSpec((1, tm, D), lambda b, i: (b, i, 0)),
                  halo(lambda i: jnp.maximum(i * (tm // HALO) - 1, 0)),
                  halo(lambda i: jnp.minimum((i + 1) * (tm // HALO), nb - 1)),
                  pl.BlockSpec((1, D), lambda b, i: (0, 0)), mod, mod,
                  pl.BlockSpec((D, 3 * W), lambda b, i: (0, 0), pipeline_mode=pl.Buffered(1)),
                  full(p["conv"]), full(p["k_k"]), full(p["ones_bd"])],
        out_specs=[out_blk] * 4,
        out_shape=[jax.ShapeDtypeStruct((B, T, W), BF16)] * 4,
        compiler_params=_cparams(("parallel", "parallel")),
        name="norm_inproj_rkv",
    )(x, x, x, g, shift, scale, p["w_in"], p["conv"], p["k_k"], p["ones_bd"])


def _lora_kernel(x_ref, g_ref, sh_ref, sc_ref, w_ref, r_ref, k_ref, v_ref, dbias, dup, ibias, iup, gup,
                 ka, rk, ones_ref, up_o, a0_o, a1_o, lw0_o, lw1_o, g_o, bonus_o):
    PW = up_o.shape[2]
    xn = _modnorm(x_ref[0], g_ref[...], sh_ref[0], sc_ref[0]).astype(BF16)
    lo = _dot(xn, w_ref[:, PW:])
    w_lo = jnp.tanh(lo[:, 0:LORA_PAD]).astype(BF16)
    a_lo = lo[:, LORA_PAD:2 * LORA_PAD].astype(BF16)
    g_lo = _sigmoid(lo[:, 2 * LORA_PAD:2 * LORA_PAD + GATE_LORA]).astype(BF16)
    up_o[0] = _dot(xn, w_ref[:, 0:PW])
    a_sum = None
    for d, (a_o, lw_o) in enumerate(((a0_o, lw0_o), (a1_o, lw1_o))):
        z = dbias[d:d + 1, :] + _dot(w_lo, dup[d])
        w_log = jnp.minimum(z, 0.0) - jnp.log(1.0 + jnp.exp(-jnp.abs(z))) - 0.5
        lw_o[0] = -jnp.exp(w_log)
        a = _sigmoid(ibias[d:d + 1, :] + _dot(a_lo, iup[d]))
        a_o[0] = a.astype(a_o.dtype)
        a_sum = a if a_sum is None else a_sum + a
    g_o[0] = _dot(g_lo, gup[...]).astype(g_o.dtype)
    r, k, v = (z[0].astype(F32) for z in (r_ref, k_ref, v_ref))
    kpair = k * (2.0 + (a_sum - 2.0) * ka[...])
    bonus_o[0] = (_head_sum(r * kpair * rk[...], ones_ref[...]) * v).astype(bonus_o.dtype)


def _lora_call(x, g, shift, scale, p, r, k, v, tm):
    B, T, D = x.shape
    W = p["k_k"].shape[1]
    PW = p["pool_scale"].shape[1]
    NL = p["w_in"].shape[1] - 3 * W
    tm = min(tm, T)
    full = lambda a: pl.BlockSpec(a.shape, lambda b, i: (0,) * a.ndim)
    mod = pl.BlockSpec((1, 1, D), lambda b, i: (b, 0, 0))
    consts = [p["dbias"], p["dup"], p["ibias"], p["iup"], p["gup"], p["k_a"], p["r_k"], p["ones_bd"]]
    blk = lambda n: pl.BlockSpec((1, tm, n), lambda b, i: (b, i, 0))
    sds = lambda n, dt: jax.ShapeDtypeStruct((B, T, n), dt)
    return pl.pallas_call(
        _lora_kernel,
        grid=(B, T // tm),
        in_specs=[blk(D), pl.BlockSpec((1, D), lambda b, i: (0, 0)), mod, mod,
                  pl.BlockSpec((D, NL), lambda b, i: (0, 3 * W // NL), pipeline_mode=pl.Buffered(1)),
                  blk(W), blk(W), blk(W)] + [full(a) for a in consts],
        out_specs=[blk(PW)] + [blk(W)] * 6,
        out_shape=[sds(PW, F32), sds(W, BF16), sds(W, BF16), sds(W, F32), sds(W, F32), sds(W, BF16),
                   sds(W, BF16)],
        compiler_params=_cparams(("parallel", "parallel")),
        name="norm_inproj_lora_pool",
    )(x, g, shift, scale, p["w_in"], r, k, v, *consts)


def _scan_masks(C):
    t = np.arange(C)[:, None]
    j = np.arange(2 * C)[None, :] % C
    fwd = [t > j, t >= j]
    rev = [t < j, t <= j]
    s = 1
    while s < C:
        blk = (t // (2 * s)) == (j // (2 * s))
        fwd.append(blk & ((t % (2 * s)) >= s) & ((j % (2 * s)) < s))
        rev.append(blk & ((t % (2 * s)) < s) & ((j % (2 * s)) >= s))
        s *= 2
    return np.stack(fwd + rev).astype(np.float32)


def _chunk_group(chains, masks, nlev, m0, m1, eye):
    n = len(chains)
    C = chains[0][0].shape[0]
    P = 2 * C
    stack = lambda xb: jnp.concatenate([xb * m0, xb * m1], axis=0)
    AR, BK, Vs = [], [], []
    for (At, Rt, Bt, Kt, v, _, _, _) in chains:
        AR.append(jnp.concatenate([At, Rt], axis=0).astype(BF16))
        BK.append(jnp.concatenate([stack(Bt.astype(BF16)), stack(Kt.astype(BF16))], axis=0))
        Vs.append(stack(v.astype(BF16)))
    sc = [_dot_nt(AR[i], BK[i]).astype(BF16) for i in range(n)]
    x1 = [_dot_nt(AR[i], chains[i][6].astype(BF16)) for i in range(n)]
    Nab = [sc[i][:C, :P] * masks[chains[i][7]] for i in range(n)]
    T = [eye + (Nab[i] * masks[chains[i][7] + 2]).astype(F32) for i in range(n)]
    for lv in range(1, nlev):
        Tb = [T[i].astype(BF16) for i in range(n)]
        Nl = [stack(Nab[i] * masks[chains[i][7] + 2 + lv]) for i in range(n)]
        TN = [_dot(Tb[i], Nl[i]).astype(BF16) for i in range(n)]
        T = [T[i] + _dot(TN[i], stack(Tb[i])) for i in range(n)]
    NV = [_dot(sc[i][:C, P:] * masks[chains[i][7]], Vs[i]) for i in range(n)]
    Us = [_dot(T[i].astype(BF16), stack((x1[i][:C] + NV[i]).astype(BF16))) for i in range(n)]
    UV = [jnp.concatenate([stack(Us[i].astype(BF16)), Vs[i]], axis=0) for i in range(n)]
    out = []
    for i in range(n):
        incl = masks[chains[i][7] + 1]
        Pr = jnp.concatenate([sc[i][C:, :P] * incl, sc[i][C:, P:] * incl], axis=1)
        y = x1[i][C:] + _dot(Pr, UV[i])
        Zn = chains[i][5] * (chains[i][6] + _dot_tn(UV[i], BK[i]))
        out.append((y, Zn))
    return out


def _scan_kernel(rf, kf, vf, kkf, af, lwf, rb, kb, vb, kkb, ab, lwb, s0f, s0b, ka_ref, tri_ref, m_ref,
                 yf_o, yb_o, sf_o, sb_o, *, C, G, nchunk):
    @pl.when(pl.program_id(2) == 0)
    def _():
        sf_o[...] = s0f[...]
        sb_o[...] = s0b[...]

    nlev = int(np.log2(C))
    nmask = 2 + nlev
    lane = lax.broadcasted_iota(jnp.int32, (1, LANES), 1)
    m0 = (lane < HEAD).astype(BF16)
    m1 = (lane >= HEAD).astype(BF16)
    ri = lax.broadcasted_iota(jnp.int32, (C, 2 * C), 0)
    ci = lax.broadcasted_iota(jnp.int32, (C, 2 * C), 1)
    eye = (ri == jnp.bitwise_and(ci, C - 1)).astype(F32)
    ka = ka_ref[...]
    dirs = ((rf, kf, vf, kkf, af, lwf, yf_o, sf_o, False), (rb, kb, vb, kkb, ab, lwb, yb_o, sb_o, True))
    for cidx in range(nchunk):
        chains, dests = [], []
        for (r_r, k_r, v_r, kk_r, a_r, lw_r, y_o, s_o, rev) in dirs:
            c = (nchunk - 1 - cidx) if rev else cidx
            rows = slice(c * C, (c + 1) * C)
            r, k, kk, a = (z[0, rows, :].astype(F32) for z in (r_r, k_r, kk_r, a_r))
            v, lw = v_r[0, rows, :], lw_r[0, rows, :]
            hi, lo = _split(lw)
            tri = tri_ref[1 if rev else 0]
            cum = _dot(tri, hi) + _dot(tri, lo)
            ea = jnp.exp(cum - lw)
            er = jnp.exp(cum)
            einv = jnp.exp(-cum)
            At = -(kk * ea)
            Rt = r * er
            Bt = (kk * a) * einv
            Kt = (k * (1.0 + (a - 1.0) * ka)) * einv
            gam = er[0:1, :] if rev else er[C - 1:C, :]
            for g in range(G):
                sl = slice(g * LANES, (g + 1) * LANES)
                chains.append((At[:, sl], Rt[:, sl], Bt[:, sl], Kt[:, sl], v[:, sl], gam[:, sl],
                               s_o[0, g], nmask if rev else 0))
            dests.append((y_o, s_o, rows))
        res = _chunk_group(chains, m_ref, nlev, m0, m1, eye)
        for d, (y_o, s_o, rows) in enumerate(dests):
            part = res[d * G:(d + 1) * G]
            for g in range(G):
                s_o[0, g] = part[g][1]
            y = part[0][0] if G == 1 else jnp.concatenate([q[0] for q in part], axis=1)
            y_o[0, rows, :] = y.astype(y_o.dtype)


def _scan_call(pre, a0, a1, lw0, lw1, s0f, s0b, ka, C, G, TB):
    r, k, v, kk = pre
    B, T, W = r.shape
    TB = min(TB, T)
    NT = T // TB
    NG = W // (G * LANES)
    GW = G * LANES
    fwd = pl.BlockSpec((1, TB, GW), lambda b, g, i: (b, i, g))
    bwd = pl.BlockSpec((1, TB, GW), lambda b, g, i: (b, NT - 1 - i, g))
    st = pl.BlockSpec((1, G, LANES, LANES), lambda b, g, i: (b, g, 0, 0))
    masks = jnp.asarray(_scan_masks(C), BF16)
    tri = jnp.asarray(np.stack([np.tril(np.ones((C, C), np.float32)),
                                np.triu(np.ones((C, C), np.float32))])).astype(BF16)
    kern = functools.partial(_scan_kernel, C=C, G=G, nchunk=TB // C)
    return pl.pallas_call(
        kern,
        grid=(B, NG, NT),
        in_specs=[fwd] * 6 + [bwd] * 6 + [st, st,
                  pl.BlockSpec((1, GW), lambda b, g, i: (0, g)),
                  pl.BlockSpec(tri.shape, lambda b, g, i: (0, 0, 0)),
                  pl.BlockSpec(masks.shape, lambda b, g, i: (0, 0, 0))],
        out_specs=[fwd, bwd, st, st],
        out_shape=[jax.ShapeDtypeStruct((B, T, W), BF16)] * 2
                  + [jax.ShapeDtypeStruct(s0f.shape, F32)] * 2,
        compiler_params=_cparams(("parallel", "parallel", "arbitrary")),
        name="wkv7_scan",
    )(r, k, v, kk, a0, lw0, r, k, v, kk, a1, lw1, s0f, s0b, ka, tri, masks)


def _window_count(pos, half, n):
    return (jnp.minimum(pos + half, n) - jnp.maximum(pos - half, 0)).astype(F32)


def _pool2d_kernel(cur, prv, nxt, pw_ref, ps_ref, o_ref, col_ref, *, tp, halo, nrows):
    i = pl.program_id(1)
    has_prev = (i > 0).astype(F32)
    has_next = (i < pl.num_programs(1) - 1).astype(F32)
    PG = pw_ref.shape[1]
    shift = GRID_COLS.bit_length() - 1
    tok = lax.broadcasted_iota(jnp.int32, (tp, PG), 0)
    col = jnp.bitwise_and(tok, GRID_COLS - 1)
    row = i * (tp // GRID_COLS) + jnp.right_shift(tok, shift)
    p = lax.broadcasted_iota(jnp.int32, (LANES, LANES), 0)
    q = lax.broadcasted_iota(jnp.int32, (LANES, LANES), 1)
    same_row = jnp.right_shift(p, shift) == jnp.right_shift(q, shift)
    dcol = jnp.bitwise_and(p, GRID_COLS - 1) - jnp.bitwise_and(q, GRID_COLS - 1)
    for gi, win in enumerate(POOL_WINS):
        half = win // 2
        cs = slice(gi * PG, (gi + 1) * PG)
        band = (same_row & (dcol <= half) & (dcol > -half)).astype(BF16)

        def colsum(x):
            hi, lo = _split(x)
            return _dot(band, hi) + _dot(band, lo)

        for c in range(halo // LANES):
            rs = slice(c * LANES, (c + 1) * LANES)
            col_ref[rs, :] = colsum(prv[0, rs, cs] * has_prev)
            rs2 = slice(halo + tp + c * LANES, halo + tp + (c + 1) * LANES)
            col_ref[rs2, :] = colsum(nxt[0, rs, cs] * has_next)
        for c in range(tp // LANES):
            rs = slice(c * LANES, (c + 1) * LANES)
            col_ref[halo + c * LANES:halo + (c + 1) * LANES, :] = colsum(cur[0, rs, cs])
        acc = None
        for dr in range(-half, half):
            part = col_ref[halo + dr * GRID_COLS:halo + dr * GRID_COLS + tp, :]
            acc = part if acc is None else acc + part
        cnt = _window_count(col, half, GRID_COLS) * _window_count(row, half, nrows)
        diff = acc * (1.0 / cnt) - cur[0, :, cs]
        o_ref[0, :, cs] = (_dot(diff.astype(BF16), pw_ref[gi]) * ps_ref[:, cs]).astype(o_ref.dtype)


def _pool2d_call(u, pool_w, pool_scale, tp):
    B, T, _ = u.shape
    PW = pool_scale.shape[1]
    halo = POOL_HALO_ROWS * GRID_COLS
    tp = min(tp, T)
    nh = T // halo
    kern = functools.partial(_pool2d_kernel, tp=tp, halo=halo, nrows=T // GRID_COLS)
    return pl.pallas_call(
        kern,
        grid=(B, T // tp),
        in_specs=[
            pl.BlockSpec((1, tp, PW), lambda b, i: (b, i, 0)),
            pl.BlockSpec((1, halo, PW), lambda b, i: (b, jnp.maximum(i * (tp // halo) - 1, 0), 0)),
            pl.BlockSpec((1, halo, PW), lambda b, i: (b, jnp.minimum((i + 1) * (tp // halo), nh - 1), 0)),
            pl.BlockSpec(pool_w.shape, lambda b, i: (0, 0, 0)),
            pl.BlockSpec((1, PW), lambda b, i: (0, 0)),
        ],
        out_specs=pl.BlockSpec((1, tp, PW), lambda b, i: (b, i, 0)),
        out_shape=jax.ShapeDtypeStruct((B, T, PW), BF16),
        scratch_shapes=[pltpu.VMEM((tp + 2 * halo, pool_w.shape[1]), F32)],
        compiler_params=_cparams(("parallel", "parallel")),
        name="pool2d",
    )(u, u, u, pool_w, pool_scale)


def _pool1d_kernel(u_ref, pw_ref, ps_ref, o_ref):
    T = u_ref.shape[1]
    PG = pw_ref.shape[1]
    p = lax.broadcasted_iota(jnp.int32, (T, T), 0)
    q = lax.broadcasted_iota(jnp.int32, (T, T), 1)
    pos = lax.broadcasted_iota(jnp.int32, (T, PG), 0)
    for gi, win in enumerate(POOL_WINS):
        half = win // 2
        cs = slice(gi * PG, (gi + 1) * PG)
        band = ((p - q <= half) & (p - q > -half)).astype(BF16)
        x = u_ref[0, :, cs]
        hi, lo = _split(x)
        m = (_dot(band, hi) + _dot(band, lo)) * (1.0 / _window_count(pos, half, T))
        o_ref[0, :, cs] = (_dot((m - x).astype(BF16), pw_ref[gi]) * ps_ref[:, cs]).astype(o_ref.dtype)


def _pool1d_call(u, pool_w, pool_scale):
    B, T, _ = u.shape
    PW = pool_scale.shape[1]
    return pl.pallas_call(
        _pool1d_kernel,
        grid=(B,),
        in_specs=[
            pl.BlockSpec((1, T, PW), lambda b: (b, 0, 0)),
            pl.BlockSpec(pool_w.shape, lambda b: (0, 0, 0)),
            pl.BlockSpec((1, PW), lambda b: (0, 0)),
        ],
        out_specs=pl.BlockSpec((1, T, PW), lambda b: (b, 0, 0)),
        out_shape=jax.ShapeDtypeStruct((B, T, PW), BF16),
        compiler_params=_cparams(("parallel",)),
        name="pool1d",
    )(u, pool_w, pool_scale)


def _outproj_kernel(x_ref, yf, yb, bonus, g, p_ref, gnw, gnb, ones_ref, gt_ref, w_ref, gf_ref, sh_ref, sc_ref,
                    o_ref, xn_o):
    W = yf.shape[2]
    ones_bd = ones_ref[...]
    y = yf[0].astype(F32) + yb[0].astype(F32)
    mu = _head_sum(y, ones_bd) * (1.0 / HEAD)
    yc = y - mu
    var = _head_sum(yc * yc, ones_bd) * (1.0 / HEAD)
    yn = yc * lax.rsqrt(var + GN_EPS) * gnw[...] + gnb[...]
    mixed = ((yn + bonus[0].astype(F32)) * g[0].astype(F32)).astype(BF16)
    acc = _dot(mixed, w_ref[0:W, :]) + _dot(p_ref[0].astype(BF16), w_ref[W:, :])
    h = x_ref[0] + gt_ref[0] * acc
    o_ref[0] = h
    xn_o[0] = _modnorm(h, gf_ref[...], sh_ref[0], sc_ref[0]).astype(xn_o.dtype)


def _outproj_call(x, yf, yb, bonus, g, p, gnw, gnb, ones_bd, gate, w, g_ffn, shift, scale, tm):
    B, T, D = x.shape
    W, PW = yf.shape[2], p.shape[2]
    tm = min(tm, T)
    blk = lambda n: pl.BlockSpec((1, tm, n), lambda b, i: (b, i, 0))
    vec = pl.BlockSpec((1, W), lambda b, i: (0, 0))
    mod = pl.BlockSpec((1, 1, D), lambda b, i: (b, 0, 0))
    return pl.pallas_call(
        _outproj_kernel,
        grid=(B, T // tm),
        in_specs=[blk(D), blk(W), blk(W), blk(W), blk(W), blk(PW), vec, vec,
                  pl.BlockSpec(ones_bd.shape, lambda b, i: (0, 0)), mod,
                  pl.BlockSpec(w.shape, lambda b, i: (0, 0), pipeline_mode=pl.Buffered(1)),
                  pl.BlockSpec((1, D), lambda b, i: (0, 0)), mod, mod],
        out_specs=[blk(D), blk(D)],
        out_shape=[jax.ShapeDtypeStruct((B, T, D), F32), jax.ShapeDtypeStruct((B, T, D), BF16)],
        compiler_params=_cparams(("parallel", "parallel")),
        name="mix_outproj_residual",
    )(x, yf, yb, bonus, g, p, gnw, gnb, ones_bd, gate, w, g_ffn, shift, scale)


def _ffn_kernel(x_ref, xn_ref, gt_ref, wg_ref, wu_ref, wd_ref, fin_ref, o_ref, acc_ref, *, final):
    j = pl.program_id(2)

    @pl.when(j == 0)
    def _():
        acc_ref[...] = jnp.zeros_like(acc_ref)

    xn = xn_ref[0]
    h = _silu(_dot(xn, wg_ref[...])) * _dot(xn, wu_ref[...])
    acc_ref[...] += _dot(h.astype(BF16), wd_ref[...])

    @pl.when(j == pl.num_programs(2) - 1)
    def _():
        h = x_ref[0] + gt_ref[0] * acc_ref[...]
        if final:
            h = h * lax.rsqrt(jnp.mean(h * h, axis=-1, keepdims=True) + RMS_EPS) * fin_ref[...]
        o_ref[0] = h


def _ffn_call(x, xn, gate, wg, wu, wd, fin, final, tm, tf):
    B, T, D = x.shape
    F = wg.shape[1]
    tm = min(tm, T)
    return pl.pallas_call(
        functools.partial(_ffn_kernel, final=final),
        grid=(B, T // tm, F // tf),
        in_specs=[
            pl.BlockSpec((1, tm, D), lambda b, i, j: (b, i, 0)),
            pl.BlockSpec((1, tm, D), lambda b, i, j: (b, i, 0)),
            pl.BlockSpec((1, 1, D), lambda b, i, j: (b, 0, 0)),
            pl.BlockSpec((D, tf), lambda b, i, j: (0, j)),
            pl.BlockSpec((D, tf), lambda b, i, j: (0, j)),
            pl.BlockSpec((tf, D), lambda b, i, j: (j, 0)),
            pl.BlockSpec((1, D), lambda b, i, j: (0, 0)),
        ],
        out_specs=pl.BlockSpec((1, tm, D), lambda b, i, j: (b, i, 0)),
        out_shape=jax.ShapeDtypeStruct((B, T, D), F32),
        scratch_shapes=[pltpu.VMEM((tm, D), F32)],
        compiler_params=_cparams(("parallel", "parallel", "arbitrary")),
        name="swiglu_residual",
    )(x, xn, gate, wg, wu, wd, fin)


def _pad_rows(w, n):
    return jnp.pad(w, ((0, 0),) * (w.ndim - 2) + ((0, n - w.shape[-2]), (0, 0)))


def _layer_params(l, w_in, conv_rkv, decay_bias, decay_up, iclr_bias, iclr_up, gate_up, k_k, k_a, r_k,
                  gn_w, gn_b, pool_w, pool_scale, w_out, ffn_gate, ffn_up, ffn_down):
    RW = k_k.shape[1]
    wl = w_in[l]
    o_w = 3 * RW
    o_a = o_w + DECAY_LORA
    o_g = o_a + ICLR_LORA
    o_p = o_g + GATE_LORA
    padc = lambda w: jnp.pad(w, ((0, 0), (0, LORA_PAD - w.shape[1])))
    w_in_r = jnp.concatenate(
        [wl[:, :o_w], wl[:, o_p:], padc(wl[:, o_w:o_a]), padc(wl[:, o_a:o_g]), wl[:, o_g:o_p]], axis=1)
    row = lambda a: a[l].reshape(1, -1)
    return {
        "w_in": w_in_r.astype(BF16),
        "conv": conv_rkv[l],
        "dbias": decay_bias[l], "dup": _pad_rows(decay_up[l], LORA_PAD).astype(BF16),
        "ibias": iclr_bias[l], "iup": _pad_rows(iclr_up[l], LORA_PAD).astype(BF16),
        "gup": gate_up[l].astype(BF16),
        "k_k": row(k_k), "k_a": row(k_a), "r_k": row(r_k), "gn_w": row(gn_w), "gn_b": row(gn_b),
        "pool_w": pool_w[l].astype(BF16), "pool_scale": row(pool_scale),
        "w_out": w_out[l].astype(BF16),
        "wg": ffn_gate[l].astype(BF16), "wu": ffn_up[l].astype(BF16), "wd": ffn_down[l].astype(BF16),
    }


_CFG = dict(tm_in=512, scan_c=64, scan_g=8, scan_tb=256, tp_pool=2048, tm_out=512,
            tm_ffn=512, tf_ffn=512, tn_ada=1024)


def _forward(cfg, x, c, ctx, c_ctx, ada_w, ada_b, norm_mix, norm_ffn, w_in, conv_rkv, decay_bias, decay_up,
             iclr_bias, iclr_up, gate_up, k_k, k_a, r_k, gn_w, gn_b, pool_w, pool_scale, w_out,
             ffn_gate, ffn_up, ffn_down, final_norm):
    B, T, D = x.shape
    L = ada_w.shape[0]
    RW = k_k.shape[1]
    npair = RW // LANES
    ones_bd = jnp.asarray(np.kron(np.eye(LANES // HEAD), np.ones((HEAD, HEAD))), BF16)
    fin = final_norm.reshape(1, D)

    cond = jnp.zeros((8, D), F32).at[:B].set(c).at[B].set(c_ctx)
    mods = _ada_call(cond, ada_w, ada_b, cfg["tn_ada"])
    mods = mods.reshape(L, 8, 6, D)

    ctx_h = ctx
    for l in range(L):
        last = l == L - 1
        p = _layer_params(l, w_in, conv_rkv, decay_bias, decay_up, iclr_bias, iclr_up, gate_up, k_k, k_a,
                          r_k, gn_w, gn_b, pool_w, pool_scale, w_out, ffn_gate, ffn_up, ffn_down)
        p["ones_bd"] = ones_bd
        m_lat = [mods[l, :B, s].reshape(B, 1, D) for s in range(6)]
        m_ctx = [jnp.broadcast_to(mods[l, B, s].reshape(1, 1, D), (B, 1, D)) for s in range(6)]
        g_mix = norm_mix[l].reshape(1, D)
        g_ffn = norm_ffn[l].reshape(1, D)
        zero = jnp.zeros((B, npair, LANES, LANES), F32)

        def mixer(h, m, s0f, s0b, need_out):
            r, k, v, kk = _rkv_call(h, g_mix, m[0], m[1], p, cfg["tm_in"])
            u, a0, a1, lw0, lw1, g, bonus = _lora_call(h, g_mix, m[0], m[1], p, r, k, v, cfg["tm_in"])
            yf, yb, sf, sb = _scan_call((r, k, v, kk), a0, a1, lw0, lw1, s0f, s0b, p["k_a"],
                                        cfg["scan_c"], cfg["scan_g"], cfg["scan_tb"])
            return u, (yf, yb, bonus, g), sf, sb

        def residuals(h, m, y, pooled, final):
            h, hn = _outproj_call(h, *y, pooled, p["gn_w"], p["gn_b"], ones_bd, m[2], p["w_out"],
                                  g_ffn, m[3], m[4], cfg["tm_out"])
            return _ffn_call(h, hn, m[5], p["wg"], p["wu"], p["wd"], fin, final, cfg["tm_ffn"], cfg["tf_ffn"])

        u_ctx, y_ctx, s_fwd, s_bwd = mixer(ctx_h, m_ctx, zero, zero, not last)
        u_lat, y_lat, _, _ = mixer(x, m_lat, s_fwd, s_bwd, True)
        p_lat = _pool2d_call(u_lat, p["pool_w"], p["pool_scale"], cfg["tp_pool"])
        x = residuals(x, m_lat, y_lat, p_lat, last)
        if not last:
            p_ctx = _pool1d_call(u_ctx, p["pool_w"], p["pool_scale"])
            ctx_h = residuals(ctx_h, m_ctx, y_ctx, p_ctx, False)
    return x


def kernel(x, c, ctx, c_ctx, ada_w, ada_b, norm_mix, norm_ffn, w_in, conv_rkv, decay_bias, decay_up,
           iclr_bias, iclr_up, gate_up, k_k, k_a, r_k, gn_w, gn_b, pool_w, pool_scale, w_out,
           ffn_gate, ffn_up, ffn_down, final_norm):
    return _forward(_CFG, x, c, ctx, c_ctx, ada_w, ada_b, norm_mix, norm_ffn, w_in, conv_rkv, decay_bias,
                    decay_up, iclr_bias, iclr_up, gate_up, k_k, k_a, r_k, gn_w, gn_b, pool_w, pool_scale,
                    w_out, ffn_gate, ffn_up, ffn_down, final_norm)
```

```python
import functools

import numpy as np
import jax
import jax.numpy as jnp
from jax import lax
from jax.experimental import pallas as pl
from jax.experimental.pallas import tpu as pltpu

F32 = jnp.float32
BF16 = jnp.bfloat16

HEAD = 64
LANES = 128
SUBLANES = 8
GRID_COLS = 64
POOL_WINS = (2, 4, 8, 16)
POOL_HALO_ROWS = 8
RMS_EPS = 1e-6
GN_EPS = 64e-5
DECAY_LORA = 96
ICLR_LORA = 96
GATE_LORA = 256
LORA_PAD = 128
VMEM_LIMIT = 56 * 1024 * 1024


def _cparams(sem):
    return pltpu.CompilerParams(dimension_semantics=sem, vmem_limit_bytes=VMEM_LIMIT)


def _tile(n, t):
    t = min(t, n)
    assert n % t == 0, (n, t)
    return t


def _dot(a, b):
    return jnp.dot(a, b, preferred_element_type=F32)


def _dot_nt(a, b):
    return lax.dot_general(a, b, (((1,), (1,)), ((), ())), preferred_element_type=F32)


def _dot_tn(a, b):
    return lax.dot_general(a, b, (((0,), (0,)), ((), ())), preferred_element_type=F32)


def _split(x):
    hi = x.astype(BF16)
    lo = (x - hi.astype(F32)).astype(BF16)
    return hi, lo


def _head_sum(x, ones_bd):
    xb = x.astype(BF16)
    parts = [_dot(xb[:, c * LANES:(c + 1) * LANES], ones_bd) for c in range(x.shape[1] // LANES)]
    return parts[0] if len(parts) == 1 else jnp.concatenate(parts, axis=1)


def _sigmoid(x):
    return 1.0 / (1.0 + jnp.exp(-x))


def _silu(x):
    return x * _sigmoid(x)


def _ada_kernel(c_ref, w_ref, b_ref, o_ref):
    c = c_ref[...]
    o_ref[0] = _dot(_silu(c).astype(BF16), w_ref[0].astype(BF16)) + b_ref[0]


def _ada_call(cond, ada_w, ada_b, tn):
    L, D, N = ada_w.shape
    R = cond.shape[0]
    return pl.pallas_call(
        _ada_kernel,
        grid=(L, N // tn),
        in_specs=[
            pl.BlockSpec((R, D), lambda l, j: (0, 0)),
            pl.BlockSpec((1, D, tn), lambda l, j: (l, 0, j)),
            pl.BlockSpec((1, 1, tn), lambda l, j: (l, 0, j)),
        ],
        out_specs=pl.BlockSpec((1, R, tn), lambda l, j: (l, 0, j)),
        out_shape=jax.ShapeDtypeStruct((L, R, N), F32),
        compiler_params=_cparams(("parallel", "parallel")),
        name="ada_modulation",
    )(cond, ada_w, ada_b.reshape(L, 1, N))


def _modnorm(x, g, shift, scale):
    ms = jnp.mean(x * x, axis=-1, keepdims=True)
    return x * lax.rsqrt(ms + RMS_EPS) * (g * (1.0 + scale)) + shift


HALO = SUBLANES


def _rkv_kernel(x_ref, xp_ref, xn_ref, g_ref, sh_ref, sc_ref, w_ref, cw, kkw, ones_ref,
                r_o, k_o, v_o, kk_o):
    i = pl.program_id(1)
    has_prev = (i > 0).astype(F32)
    has_next = (i < pl.num_programs(1) - 1).astype(F32)
    tm = x_ref.shape[1]
    W = r_o.shape[2]
    norm = lambda x: _modnorm(x, g_ref[...], sh_ref[0], sc_ref[0])
    xn = jnp.concatenate([norm(xp_ref[0]) * has_prev, norm(x_ref[0]), norm(xn_ref[0]) * has_next], axis=0)
    xn = xn.astype(BF16)
    rows = slice(HALO, HALO + tm)

    def conv_proj(c0):
        u = _dot(xn, w_ref[:, c0:c0 + W])
        w = cw[:, c0:c0 + W]
        return (pltpu.roll(u, 1, 0)[rows] * w[0:1, :] + u[rows] * w[1:2, :]
                + pltpu.roll(u, tm + 2 * HALO - 1, 0)[rows] * w[2:3, :])

    r_o[0] = conv_proj(0).astype(r_o.dtype)
    k = conv_proj(W)
    k_o[0] = k.astype(k_o.dtype)
    v_o[0] = conv_proj(2 * W).astype(v_o.dtype)
    kk = k * kkw[...]
    kk = kk * lax.rsqrt(jnp.maximum(_head_sum(kk * kk, ones_ref[...]), 1e-24))
    kk_o[0] = kk.astype(kk_o.dtype)


def _rkv_call(x, g, shift, scale, p, tm):
    B, T, D = x.shape
    W = p["k_k"].shape[1]
    tm = _tile(T, tm)
    nb = T // HALO
    halo = lambda f: pl.BlockSpec((1, HALO, D), lambda b, i: (b, f(i), 0))
    full = lambda a: pl.BlockSpec(a.shape, lambda b, i: (0,) * a.ndim)
    mod = pl.BlockSpec((1, 1, D), lambda b, i: (b, 0, 0))
    out_blk = pl.BlockSpec((1, tm, W), lambda b, i: (b, i, 0))
    return pl.pallas_call(
        _rkv_kernel,
        grid=(B, T // tm),
        in_specs=[pl.BlockSpec((1, tm, D), lambda b, i: (b, i, 0)),
                  halo(lambda i: jnp.maximum(i * (tm // HALO) - 1, 0)),
                  halo(lambda i: jnp.minimum((i + 1) * (tm // HALO), nb - 1)),
                  pl.BlockSpec((1, D), lambda b, i: (0, 0)), mod, mod,
                  pl.BlockSpec((D, 3 * W), lambda b, i: (0, 0), pipeline_mode=pl.Buffered(1)),
                  full(p["conv"]), full(p["k_k"]), full(p["ones_bd"])],
        out_specs=[out_blk] * 4,
        out_shape=[jax.ShapeDtypeStruct((B, T, W), BF16)] * 4,
        compiler_params=_cparams(("parallel", "parallel")),
        name="norm_inproj_rkv",
    )(x, x, x, g, shift, scale, p["w_in"], p["conv"], p["k_k"], p["ones_bd"])


def _lora_kernel(x_ref, g_ref, sh_ref, sc_ref, w_ref, r_ref, k_ref, v_ref, dbias, dup, ibias, iup, gup,
                 ka, rk, ones_ref, up_o, a0_o, a1_o, lw0_o, lw1_o, g_o, bonus_o):
    PW = up_o.shape[2]
    xn = _modnorm(x_ref[0], g_ref[...], sh_ref[0], sc_ref[0]).astype(BF16)
    lo = _dot(xn, w_ref[:, PW:])
    w_lo = jnp.tanh(lo[:, 0:LORA_PAD]).astype(BF16)
    a_lo = lo[:, LORA_PAD:2 * LORA_PAD].astype(BF16)
    g_lo = _sigmoid(lo[:, 2 * LORA_PAD:2 * LORA_PAD + GATE_LORA]).astype(BF16)
    up_o[0] = _dot(xn, w_ref[:, 0:PW])
    a_sum = None
    for d, (a_o, lw_o) in enumerate(((a0_o, lw0_o), (a1_o, lw1_o))):
        z = dbias[d:d + 1, :] + _dot(w_lo, dup[d])
        w_log = jnp.minimum(z, 0.0) - jnp.log(1.0 + jnp.exp(-jnp.abs(z))) - 0.5
        lw_o[0] = -jnp.exp(w_log)
        a = _sigmoid(ibias[d:d + 1, :] + _dot(a_lo, iup[d]))
        a_o[0] = a.astype(a_o.dtype)
        a_sum = a if a_sum is None else a_sum + a
    g_o[0] = _dot(g_lo, gup[...]).astype(g_o.dtype)
    r, k, v = (z[0].astype(F32) for z in (r_ref, k_ref, v_ref))
    kpair = k * (2.0 + (a_sum - 2.0) * ka[...])
    bonus_o[0] = (_head_sum(r * kpair * rk[...], ones_ref[...]) * v).astype(bonus_o.dtype)


def _lora_call(x, g, shift, scale, p, r, k, v, tm):
    B, T, D = x.shape
    W = p["k_k"].shape[1]
    PW = p["pool_scale"].shape[1]
    NL = p["w_in"].shape[1] - 3 * W
    tm = _tile(T, tm)
    full = lambda a: pl.BlockSpec(a.shape, lambda b, i: (0,) * a.ndim)
    mod = pl.BlockSpec((1, 1, D), lambda b, i: (b, 0, 0))
    consts = [p["dbias"], p["dup"], p["ibias"], p["iup"], p["gup"], p["k_a"], p["r_k"], p["ones_bd"]]
    blk = lambda n: pl.BlockSpec((1, tm, n), lambda b, i: (b, i, 0))
    sds = lambda n, dt: jax.ShapeDtypeStruct((B, T, n), dt)
    return pl.pallas_call(
        _lora_kernel,
        grid=(B, T // tm),
        in_specs=[blk(D), pl.BlockSpec((1, D), lambda b, i: (0, 0)), mod, mod,
                  pl.BlockSpec((D, NL), lambda b, i: (0, 3 * W // NL), pipeline_mode=pl.Buffered(1)),
                  blk(W), blk(W), blk(W)] + [full(a) for a in consts],
        out_specs=[blk(PW)] + [blk(W)] * 6,
        out_shape=[sds(PW, F32), sds(W, BF16), sds(W, BF16), sds(W, F32), sds(W, F32), sds(W, BF16),
                   sds(W, BF16)],
        compiler_params=_cparams(("parallel", "parallel")),
        name="norm_inproj_lora_pool",
    )(x, g, shift, scale, p["w_in"], r, k, v, *consts)


def _scan_masks(C):
    t = np.arange(C)[:, None]
    j = np.arange(2 * C)[None, :] % C
    fwd = [t > j, t >= j]
    rev = [t < j, t <= j]
    s = 1
    while s < C:
        blk = (t // (2 * s)) == (j // (2 * s))
        fwd.append(blk & ((t % (2 * s)) >= s) & ((j % (2 * s)) < s))
        rev.append(blk & ((t % (2 * s)) < s) & ((j % (2 * s)) >= s))
        s *= 2
    return np.stack(fwd + rev).astype(np.float32)


def _chunk_group(chains, masks, nlev, m0, m1, eye):
    n = len(chains)
    C = chains[0][0].shape[0]
    P = 2 * C
    stack = lambda xb: jnp.concatenate([xb * m0, xb * m1], axis=0)
    AR, BK, Vs = [], [], []
    for (At, Rt, Bt, Kt, v, _, _, _) in chains:
        AR.append(jnp.concatenate([At, Rt], axis=0).astype(BF16))
        BK.append(jnp.concatenate([stack(Bt.astype(BF16)), stack(Kt.astype(BF16))], axis=0))
        Vs.append(stack(v.astype(BF16)))
    sc = [_dot_nt(AR[i], BK[i]).astype(BF16) for i in range(n)]
    x1 = [_dot_nt(AR[i], chains[i][6].astype(BF16)) for i in range(n)]
    Nab = [sc[i][:C, :P] * masks[chains[i][7]] for i in range(n)]
    T = [eye + (Nab[i] * masks[chains[i][7] + 2]).astype(F32) for i in range(n)]
    for lv in range(1, nlev):
        Tb = [T[i].astype(BF16) for i in range(n)]
        Nl = [stack(Nab[i] * masks[chains[i][7] + 2 + lv]) for i in range(n)]
        TN = [_dot(Tb[i], Nl[i]).astype(BF16) for i in range(n)]
        T = [T[i] + _dot(TN[i], stack(Tb[i])) for i in range(n)]
    NV = [_dot(sc[i][:C, P:] * masks[chains[i][7]], Vs[i]) for i in range(n)]
    Us = [_dot(T[i].astype(BF16), stack((x1[i][:C] + NV[i]).astype(BF16))) for i in range(n)]
    UV = [jnp.concatenate([stack(Us[i].astype(BF16)), Vs[i]], axis=0) for i in range(n)]
    out = []
    for i in range(n):
        incl = masks[chains[i][7] + 1]
        Pr = jnp.concatenate([sc[i][C:, :P] * incl, sc[i][C:, P:] * incl], axis=1)
        y = x1[i][C:] + _dot(Pr, UV[i])
        Zn = chains[i][5] * (chains[i][6] + _dot_tn(UV[i], BK[i]))
        out.append((y, Zn))
    return out


def _scan_kernel(rf, kf, vf, kkf, af, lwf, rb, kb, vb, kkb, ab, lwb, s0f, s0b, ka_ref, tri_ref, m_ref,
                 yf_o, yb_o, sf_o, sb_o, *, C, G, nchunk):
    @pl.when(pl.program_id(2) == 0)
    def _():
        sf_o[...] = s0f[...]
        sb_o[...] = s0b[...]

    nlev = int(np.log2(C))
    nmask = 2 + nlev
    lane = lax.broadcasted_iota(jnp.int32, (1, LANES), 1)
    m0 = (lane < HEAD).astype(BF16)
    m1 = (lane >= HEAD).astype(BF16)
    ri = lax.broadcasted_iota(jnp.int32, (C, 2 * C), 0)
    ci = lax.broadcasted_iota(jnp.int32, (C, 2 * C), 1)
    eye = (ri == jnp.bitwise_and(ci, C - 1)).astype(F32)
    ka = ka_ref[...]
    dirs = ((rf, kf, vf, kkf, af, lwf, yf_o, sf_o, False), (rb, kb, vb, kkb, ab, lwb, yb_o, sb_o, True))
    for cidx in range(nchunk):
        chains, dests = [], []
        for (r_r, k_r, v_r, kk_r, a_r, lw_r, y_o, s_o, rev) in dirs:
            c = (nchunk - 1 - cidx) if rev else cidx
            rows = slice(c * C, (c + 1) * C)
            r, k, kk, a = (z[0, rows, :].astype(F32) for z in (r_r, k_r, kk_r, a_r))
            v, lw = v_r[0, rows, :], lw_r[0, rows, :]
            hi, lo = _split(lw)
            tri = tri_ref[1 if rev else 0]
            cum = _dot(tri, hi) + _dot(tri, lo)
            ea = jnp.exp(cum - lw)
            er = jnp.exp(cum)
            einv = jnp.exp(-cum)
            At = -(kk * ea)
            Rt = r * er
            Bt = (kk * a) * einv
            Kt = (k * (1.0 + (a - 1.0) * ka)) * einv
            gam = er[0:1, :] if rev else er[C - 1:C, :]
            for g in range(G):
                sl = slice(g * LANES, (g + 1) * LANES)
                chains.append((At[:, sl], Rt[:, sl], Bt[:, sl], Kt[:, sl], v[:, sl], gam[:, sl],
                               s_o[0, g], nmask if rev else 0))
            dests.append((y_o, s_o, rows))
        res = _chunk_group(chains, m_ref, nlev, m0, m1, eye)
        for d, (y_o, s_o, rows) in enumerate(dests):
            part = res[d * G:(d + 1) * G]
            for g in range(G):
                s_o[0, g] = part[g][1]
            y = part[0][0] if G == 1 else jnp.concatenate([q[0] for q in part], axis=1)
            y_o[0, rows, :] = y.astype(y_o.dtype)


def _scan_call(pre, a0, a1, lw0, lw1, s0f, s0b, ka, C, G, TB):
    r, k, v, kk = pre
    B, T, W = r.shape
    TB = _tile(T, TB)
    assert TB % C == 0 and W % (G * LANES) == 0
    NT = T // TB
    NG = W // (G * LANES)
    GW = G * LANES
    fwd = pl.BlockSpec((1, TB, GW), lambda b, g, i: (b, i, g))
    bwd = pl.BlockSpec((1, TB, GW), lambda b, g, i: (b, NT - 1 - i, g))
    st = pl.BlockSpec((1, G, LANES, LANES), lambda b, g, i: (b, g, 0, 0))
    masks = jnp.asarray(_scan_masks(C), BF16)
    tri = jnp.asarray(np.stack([np.tril(np.ones((C, C), np.float32)),
                                np.triu(np.ones((C, C), np.float32))])).astype(BF16)
    kern = functools.partial(_scan_kernel, C=C, G=G, nchunk=TB // C)
    return pl.pallas_call(
        kern,
        grid=(B, NG, NT),
        in_specs=[fwd] * 6 + [bwd] * 6 + [st, st,
                  pl.BlockSpec((1, GW), lambda b, g, i: (0, g)),
                  pl.BlockSpec(tri.shape, lambda b, g, i: (0, 0, 0)),
                  pl.BlockSpec(masks.shape, lambda b, g, i: (0, 0, 0))],
        out_specs=[fwd, bwd, st, st],
        out_shape=[jax.ShapeDtypeStruct((B, T, W), BF16)] * 2
                  + [jax.ShapeDtypeStruct(s0f.shape, F32)] * 2,
        compiler_params=_cparams(("parallel", "parallel", "arbitrary")),
        name="wkv7_scan",
    )(r, k, v, kk, a0, lw0, r, k, v, kk, a1, lw1, s0f, s0b, ka, tri, masks)


def _window_count(pos, half, n):
    return (jnp.minimum(pos + half, n) - jnp.maximum(pos - half, 0)).astype(F32)


def _pool2d_kernel(cur, prv, nxt, pw_ref, ps_ref, o_ref, col_ref, *, tp, halo, nrows):
    i = pl.program_id(1)
    has_prev = (i > 0).astype(F32)
    has_next = (i < pl.num_programs(1) - 1).astype(F32)
    PG = pw_ref.shape[1]
    shift = GRID_COLS.bit_length() - 1
    tok = lax.broadcasted_iota(jnp.int32, (tp, PG), 0)
    col = jnp.bitwise_and(tok, GRID_COLS - 1)
    row = i * (tp // GRID_COLS) + jnp.right_shift(tok, shift)
    p = lax.broadcasted_iota(jnp.int32, (LANES, LANES), 0)
    q = lax.broadcasted_iota(jnp.int32, (LANES, LANES), 1)
    same_row = jnp.right_shift(p, shift) == jnp.right_shift(q, shift)
    dcol = jnp.bitwise_and(p, GRID_COLS - 1) - jnp.bitwise_and(q, GRID_COLS - 1)
    for gi, win in enumerate(POOL_WINS):
        half = win // 2
        cs = slice(gi * PG, (gi + 1) * PG)
        band = (same_row & (dcol <= half) & (dcol > -half)).astype(BF16)

        def colsum(x):
            hi, lo = _split(x)
            return _dot(band, hi) + _dot(band, lo)

        for c in range(halo // LANES):
            rs = slice(c * LANES, (c + 1) * LANES)
            col_ref[rs, :] = colsum(prv[0, rs, cs] * has_prev)
            rs2 = slice(halo + tp + c * LANES, halo + tp + (c + 1) * LANES)
            col_ref[rs2, :] = colsum(nxt[0, rs, cs] * has_next)
        for c in range(tp // LANES):
            rs = slice(c * LANES, (c + 1) * LANES)
            col_ref[halo + c * LANES:halo + (c + 1) * LANES, :] = colsum(cur[0, rs, cs])
        acc = None
        for dr in range(-half, half):
            part = col_ref[halo + dr * GRID_COLS:halo + dr * GRID_COLS + tp, :]
            acc = part if acc is None else acc + part
        cnt = _window_count(col, half, GRID_COLS) * _window_count(row, half, nrows)
        diff = acc * (1.0 / cnt) - cur[0, :, cs]
        o_ref[0, :, cs] = (_dot(diff.astype(BF16), pw_ref[gi]) * ps_ref[:, cs]).astype(o_ref.dtype)


def _pool2d_call(u, pool_w, pool_scale, tp):
    B, T, _ = u.shape
    PW = pool_scale.shape[1]
    halo = POOL_HALO_ROWS * GRID_COLS
    tp = _tile(T, tp)
    assert tp % halo == 0 and T % halo == 0
    nh = T // halo
    kern = functools.partial(_pool2d_kernel, tp=tp, halo=halo, nrows=T // GRID_COLS)
    return pl.pallas_call(
        kern,
        grid=(B, T // tp),
        in_specs=[
            pl.BlockSpec((1, tp, PW), lambda b, i: (b, i, 0)),
            pl.BlockSpec((1, halo, PW), lambda b, i: (b, jnp.maximum(i * (tp // halo) - 1, 0), 0)),
            pl.BlockSpec((1, halo, PW), lambda b, i: (b, jnp.minimum((i + 1) * (tp // halo), nh - 1), 0)),
            pl.BlockSpec(pool_w.shape, lambda b, i: (0, 0, 0)),
            pl.BlockSpec((1, PW), lambda b, i: (0, 0)),
        ],
        out_specs=pl.BlockSpec((1, tp, PW), lambda b, i: (b, i, 0)),
        out_shape=jax.ShapeDtypeStruct((B, T, PW), BF16),
        scratch_shapes=[pltpu.VMEM((tp + 2 * halo, pool_w.shape[1]), F32)],
        compiler_params=_cparams(("parallel", "parallel")),
        name="pool2d",
    )(u, u, u, pool_w, pool_scale)


def _pool1d_kernel(u_ref, pw_ref, ps_ref, o_ref):
    T = u_ref.shape[1]
    PG = pw_ref.shape[1]
    p = lax.broadcasted_iota(jnp.int32, (T, T), 0)
    q = lax.broadcasted_iota(jnp.int32, (T, T), 1)
    pos = lax.broadcasted_iota(jnp.int32, (T, PG), 0)
    for gi, win in enumerate(POOL_WINS):
        half = win // 2
        cs = slice(gi * PG, (gi + 1) * PG)
        band = ((p - q <= half) & (p - q > -half)).astype(BF16)
        x = u_ref[0, :, cs]
        hi, lo = _split(x)
        m = (_dot(band, hi) + _dot(band, lo)) * (1.0 / _window_count(pos, half, T))
        o_ref[0, :, cs] = (_dot((m - x).astype(BF16), pw_ref[gi]) * ps_ref[:, cs]).astype(o_ref.dtype)


def _pool1d_call(u, pool_w, pool_scale):
    B, T, _ = u.shape
    PW = pool_scale.shape[1]
    return pl.pallas_call(
        _pool1d_kernel,
        grid=(B,),
        in_specs=[
            pl.BlockSpec((1, T, PW), lambda b: (b, 0, 0)),
            pl.BlockSpec(pool_w.shape, lambda b: (0, 0, 0)),
            pl.BlockSpec((1, PW), lambda b: (0, 0)),
        ],
        out_specs=pl.BlockSpec((1, T, PW), lambda b: (b, 0, 0)),
        out_shape=jax.ShapeDtypeStruct((B, T, PW), BF16),
        compiler_params=_cparams(("parallel",)),
        name="pool1d",
    )(u, pool_w, pool_scale)


def _outproj_kernel(x_ref, yf, yb, bonus, g, p_ref, gnw, gnb, ones_ref, gt_ref, w_ref, gf_ref, sh_ref, sc_ref,
                    o_ref, xn_o):
    W = yf.shape[2]
    ones_bd = ones_ref[...]
    y = yf[0].astype(F32) + yb[0].astype(F32)
    mu = _head_sum(y, ones_bd) * (1.0 / HEAD)
    yc = y - mu
    var = _head_sum(yc * yc, ones_bd) * (1.0 / HEAD)
    yn = yc * lax.rsqrt(var + GN_EPS) * gnw[...] + gnb[...]
    mixed = ((yn + bonus[0].astype(F32)) * g[0].astype(F32)).astype(BF16)
    acc = _dot(mixed, w_ref[0:W, :]) + _dot(p_ref[0].astype(BF16), w_ref[W:, :])
    h = x_ref[0] + gt_ref[0] * acc
    o_ref[0] = h
    xn_o[0] = _modnorm(h, gf_ref[...], sh_ref[0], sc_ref[0]).astype(xn_o.dtype)


def _outproj_call(x, yf, yb, bonus, g, p, gnw, gnb, ones_bd, gate, w, g_ffn, shift, scale, tm):
    B, T, D = x.shape
    W, PW = yf.shape[2], p.shape[2]
    tm = _tile(T, tm)
    blk = lambda n: pl.BlockSpec((1, tm, n), lambda b, i: (b, i, 0))
    vec = pl.BlockSpec((1, W), lambda b, i: (0, 0))
    mod = pl.BlockSpec((1, 1, D), lambda b, i: (b, 0, 0))
    return pl.pallas_call(
        _outproj_kernel,
        grid=(B, T // tm),
        in_specs=[blk(D), blk(W), blk(W), blk(W), blk(W), blk(PW), vec, vec,
                  pl.BlockSpec(ones_bd.shape, lambda b, i: (0, 0)), mod,
                  pl.BlockSpec(w.shape, lambda b, i: (0, 0), pipeline_mode=pl.Buffered(1)),
                  pl.BlockSpec((1, D), lambda b, i: (0, 0)), mod, mod],
        out_specs=[blk(D), blk(D)],
        out_shape=[jax.ShapeDtypeStruct((B, T, D), F32), jax.ShapeDtypeStruct((B, T, D), BF16)],
        compiler_params=_cparams(("parallel", "parallel")),
        name="mix_outproj_residual",
    )(x, yf, yb, bonus, g, p, gnw, gnb, ones_bd, gate, w, g_ffn, shift, scale)


def _ffn_kernel(x_ref, xn_ref, gt_ref, wg_ref, wu_ref, wd_ref, fin_ref, o_ref, acc_ref, *, final):
    j = pl.program_id(2)

    @pl.when(j == 0)
    def _():
        acc_ref[...] = jnp.zeros_like(acc_ref)

    xn = xn_ref[0]
    h = _silu(_dot(xn, wg_ref[...])) * _dot(xn, wu_ref[...])
    acc_ref[...] += _dot(h.astype(BF16), wd_ref[...])

    @pl.when(j == pl.num_programs(2) - 1)
    def _():
        h = x_ref[0] + gt_ref[0] * acc_ref[...]
        if final:
            h = h * lax.rsqrt(jnp.mean(h * h, axis=-1, keepdims=True) + RMS_EPS) * fin_ref[...]
        o_ref[0] = h


def _ffn_call(x, xn, gate, wg, wu, wd, fin, final, tm, tf):
    B, T, D = x.shape
    F = wg.shape[1]
    tm = _tile(T, tm)
    return pl.pallas_call(
        functools.partial(_ffn_kernel, final=final),
        grid=(B, T // tm, F // tf),
        in_specs=[
            pl.BlockSpec((1, tm, D), lambda b, i, j: (b, i, 0)),
            pl.BlockSpec((1, tm, D), lambda b, i, j: (b, i, 0)),
            pl.BlockSpec((1, 1, D), lambda b, i, j: (b, 0, 0)),
            pl.BlockSpec((D, tf), lambda b, i, j: (0, j)),
            pl.BlockSpec((D, tf), lambda b, i, j: (0, j)),
            pl.BlockSpec((tf, D), lambda b, i, j: (j, 0)),
            pl.BlockSpec((1, D), lambda b, i, j: (0, 0)),
        ],
        out_specs=pl.BlockSpec((1, tm, D), lambda b, i, j: (b, i, 0)),
        out_shape=jax.ShapeDtypeStruct((B, T, D), F32),
        scratch_shapes=[pltpu.VMEM((tm, D), F32)],
        compiler_params=_cparams(("parallel", "parallel", "arbitrary")),
        name="swiglu_residual",
    )(x, xn, gate, wg, wu, wd, fin)


def _pad_rows(w, n):
    return jnp.pad(w, ((0, 0),) * (w.ndim - 2) + ((0, n - w.shape[-2]), (0, 0)))


def _layer_params(l, w_in, conv_rkv, decay_bias, decay_up, iclr_bias, iclr_up, gate_up, k_k, k_a, r_k,
                  gn_w, gn_b, pool_w, pool_scale, w_out, ffn_gate, ffn_up, ffn_down):
    RW = k_k.shape[1]
    wl = w_in[l]
    o_w = 3 * RW
    o_a = o_w + DECAY_LORA
    o_g = o_a + ICLR_LORA
    o_p = o_g + GATE_LORA
    padc = lambda w: jnp.pad(w, ((0, 0), (0, LORA_PAD - w.shape[1])))
    w_in_r = jnp.concatenate(
        [wl[:, :o_w], wl[:, o_p:], padc(wl[:, o_w:o_a]), padc(wl[:, o_a:o_g]), wl[:, o_g:o_p]], axis=1)
    row = lambda a: a[l].reshape(1, -1)
    return {
        "w_in": w_in_r.astype(BF16),
        "conv": conv_rkv[l],
        "dbias": decay_bias[l], "dup": _pad_rows(decay_up[l], LORA_PAD).astype(BF16),
        "ibias": iclr_bias[l], "iup": _pad_rows(iclr_up[l], LORA_PAD).astype(BF16),
        "gup": gate_up[l].astype(BF16),
        "k_k": row(k_k), "k_a": row(k_a), "r_k": row(r_k), "gn_w": row(gn_w), "gn_b": row(gn_b),
        "pool_w": pool_w[l].astype(BF16), "pool_scale": row(pool_scale),
        "w_out": w_out[l].astype(BF16),
        "wg": ffn_gate[l].astype(BF16), "wu": ffn_up[l].astype(BF16), "wd": ffn_down[l].astype(BF16),
    }


_CFG = dict(tm_in=512, scan_c=64, scan_g=8, scan_tb=256, tp_pool=2048, tm_out=512,
            tm_ffn=512, tf_ffn=512, tn_ada=1024)


def _forward(cfg, x, c, ctx, c_ctx, ada_w, ada_b, norm_mix, norm_ffn, w_in, conv_rkv, decay_bias, decay_up,
             iclr_bias, iclr_up, gate_up, k_k, k_a, r_k, gn_w, gn_b, pool_w, pool_scale, w_out,
             ffn_gate, ffn_up, ffn_down, final_norm):
    B, T, D = x.shape
    Tc = ctx.shape[1]
    L = ada_w.shape[0]
    RW = k_k.shape[1]
    npair = RW // LANES
    assert RW % LANES == 0 and B + 1 <= SUBLANES
    ones_bd = jnp.asarray(np.kron(np.eye(LANES // HEAD), np.ones((HEAD, HEAD))), BF16)
    fin = final_norm.reshape(1, D)

    cond = jnp.zeros((SUBLANES, D), F32).at[:B].set(c).at[B].set(c_ctx)
    mods = _ada_call(cond, ada_w, ada_b, cfg["tn_ada"])
    mods = mods.reshape(L, SUBLANES, 6, D)

    ctx_h = ctx
    for l in range(L):
        last = l == L - 1
        p = _layer_params(l, w_in, conv_rkv, decay_bias, decay_up, iclr_bias, iclr_up, gate_up, k_k, k_a,
                          r_k, gn_w, gn_b, pool_w, pool_scale, w_out, ffn_gate, ffn_up, ffn_down)
        p["ones_bd"] = ones_bd
        m_lat = [mods[l, :B, s].reshape(B, 1, D) for s in range(6)]
        m_ctx = [jnp.broadcast_to(mods[l, B, s].reshape(1, 1, D), (B, 1, D)) for s in range(6)]
        g_mix = norm_mix[l].reshape(1, D)
        g_ffn = norm_ffn[l].reshape(1, D)
        zero = jnp.zeros((B, npair, LANES, LANES), F32)

        def mixer(h, m, s0f, s0b):
            r, k, v, kk = _rkv_call(h, g_mix, m[0], m[1], p, cfg["tm_in"])
            u, a0, a1, lw0, lw1, g, bonus = _lora_call(h, g_mix, m[0], m[1], p, r, k, v, cfg["tm_in"])
            yf, yb, sf, sb = _scan_call((r, k, v, kk), a0, a1, lw0, lw1, s0f, s0b, p["k_a"],
                                        cfg["scan_c"], cfg["scan_g"], cfg["scan_tb"])
            return u, (yf, yb, bonus, g), sf, sb

        def residuals(h, m, y, pooled, final):
            h, hn = _outproj_call(h, *y, pooled, p["gn_w"], p["gn_b"], ones_bd, m[2], p["w_out"],
                                  g_ffn, m[3], m[4], cfg["tm_out"])
            return _ffn_call(h, hn, m[5], p["wg"], p["wu"], p["wd"], fin, final, cfg["tm_ffn"], cfg["tf_ffn"])

        u_ctx, y_ctx, s_fwd, s_bwd = mixer(ctx_h, m_ctx, zero, zero)
        u_lat, y_lat, _, _ = mixer(x, m_lat, s_fwd, s_bwd)
        p_lat = _pool2d_call(u_lat, p["pool_w"], p["pool_scale"], cfg["tp_pool"])
        x = residuals(x, m_lat, y_lat, p_lat, last)
        if not last:
            p_ctx = _pool1d_call(u_ctx, p["pool_w"], p["pool_scale"])
            flat = lambda a: a.reshape(1, B * Tc, a.shape[2])
            ctx_h = residuals(flat(ctx_h), [mm[:1] for mm in m_ctx], tuple(flat(a) for a in y_ctx),
                              flat(p_ctx), False).reshape(B, Tc, D)
    return x


def kernel(x, c, ctx, c_ctx, ada_w, ada_b, norm_mix, norm_ffn, w_in, conv_rkv, decay_bias, decay_up,
           iclr_bias, iclr_up, gate_up, k_k, k_a, r_k, gn_w, gn_b, pool_w, pool_scale, w_out,
           ffn_gate, ffn_up, ffn_down, final_norm):
    return _forward(_CFG, x, c, ctx, c_ctx, ada_w, ada_b, norm_mix, norm_ffn, w_in, conv_rkv, decay_bias,
                    decay_up, iclr_bias, iclr_up, gate_up, k_k, k_a, r_k, gn_w, gn_b, pool_w, pool_scale,
                    w_out, ffn_gate, ffn_up, ffn_down, final_norm)
```

```python
import functools

import numpy as np
import jax
import jax.numpy as jnp
from jax import lax
from jax.experimental import pallas as pl
from jax.experimental.pallas import tpu as pltpu

F32 = jnp.float32
BF16 = jnp.bfloat16

HEAD = 64
LANES = 128
SUBLANES = 8
GRID_COLS = 64
POOL_WINS = (2, 4, 8, 16)
POOL_HALO_ROWS = 8
RMS_EPS = 1e-6
GN_EPS = 64e-5
KK_NORM_FLOOR = 1e-24
DECAY_SCALE = float(np.exp(-0.5))
DECAY_LORA = 96
ICLR_LORA = 96
GATE_LORA = 256
LORA_PAD = 128
VMEM_LIMIT = 56 * 1024 * 1024


def _cparams(sem):
    return pltpu.CompilerParams(dimension_semantics=sem, vmem_limit_bytes=VMEM_LIMIT)


def _tile(n, t):
    t = min(t, n)
    assert n % t == 0, (n, t)
    return t


def _dot(a, b):
    return jnp.dot(a, b, preferred_element_type=F32)


def _dot_nt(a, b):
    return lax.dot_general(a, b, (((1,), (1,)), ((), ())), preferred_element_type=F32)


def _dot_tn(a, b):
    return lax.dot_general(a, b, (((0,), (0,)), ((), ())), preferred_element_type=F32)


def _split(x):
    hi = x.astype(BF16)
    lo = (x - hi.astype(F32)).astype(BF16)
    return hi, lo


def _head_sum(x, ones_bd):
    xb = x.astype(BF16)
    parts = [_dot(xb[:, c * LANES:(c + 1) * LANES], ones_bd) for c in range(x.shape[1] // LANES)]
    return parts[0] if len(parts) == 1 else jnp.concatenate(parts, axis=1)


def _sigmoid(x):
    return 1.0 / (1.0 + jnp.exp(-x))


def _silu(x):
    return x * _sigmoid(x)


def _ada_kernel(c_ref, w_ref, b_ref, o_ref):
    c = c_ref[...]
    o_ref[0] = _dot(_silu(c).astype(BF16), w_ref[0].astype(BF16)) + b_ref[0]


def _ada_call(cond, ada_w, ada_b, tn):
    L, D, N = ada_w.shape
    R = cond.shape[0]
    return pl.pallas_call(
        _ada_kernel,
        grid=(L, N // tn),
        in_specs=[
            pl.BlockSpec((R, D), lambda l, j: (0, 0)),
            pl.BlockSpec((1, D, tn), lambda l, j: (l, 0, j)),
            pl.BlockSpec((1, 1, tn), lambda l, j: (l, 0, j)),
        ],
        out_specs=pl.BlockSpec((1, R, tn), lambda l, j: (l, 0, j)),
        out_shape=jax.ShapeDtypeStruct((L, R, N), F32),
        compiler_params=_cparams(("parallel", "parallel")),
        name="ada_modulation",
    )(cond, ada_w, ada_b.reshape(L, 1, N))


def _modnorm(x, g, shift, scale):
    ms = jnp.mean(x * x, axis=-1, keepdims=True)
    return x * lax.rsqrt(ms + RMS_EPS) * (g * (1.0 + scale)) + shift


HALO = SUBLANES


def _rkv_kernel(x_ref, xp_ref, xn_ref, g_ref, sh_ref, sc_ref, w_ref, cw, kkw, ones_ref,
                r_o, k_o, v_o, kk_o, xn_o):
    i = pl.program_id(1)
    has_prev = (i > 0).astype(F32)
    has_next = (i < pl.num_programs(1) - 1).astype(F32)
    tm = x_ref.shape[1]
    W = r_o.shape[2]
    norm = lambda x: _modnorm(x, g_ref[...], sh_ref[0], sc_ref[0])
    xc = norm(x_ref[0])
    xn_o[0] = xc.astype(xn_o.dtype)
    xn = jnp.concatenate([norm(xp_ref[0]) * has_prev, xc, norm(xn_ref[0]) * has_next], axis=0)
    xn = xn.astype(BF16)
    rows = slice(HALO, HALO + tm)

    def conv_proj(c0):
        u = _dot(xn, w_ref[:, c0:c0 + W])
        w = cw[:, c0:c0 + W]
        return (pltpu.roll(u, 1, 0)[rows] * w[0:1, :] + u[rows] * w[1:2, :]
                + pltpu.roll(u, tm + 2 * HALO - 1, 0)[rows] * w[2:3, :])

    r_o[0] = conv_proj(0).astype(r_o.dtype)
    k = conv_proj(W)
    k_o[0] = k.astype(k_o.dtype)
    v_o[0] = conv_proj(2 * W).astype(v_o.dtype)
    kk = k * kkw[...]
    kk = kk * lax.rsqrt(jnp.maximum(_head_sum(kk * kk, ones_ref[...]), KK_NORM_FLOOR))
    kk_o[0] = kk.astype(kk_o.dtype)


def _rkv_call(x, g, shift, scale, p, tm):
    B, T, D = x.shape
    W = p["k_k"].shape[1]
    tm = _tile(T, tm)
    nb = T // HALO
    halo = lambda f: pl.BlockSpec((1, HALO, D), lambda b, i: (b, f(i), 0))
    full = lambda a: pl.BlockSpec(a.shape, lambda b, i: (0,) * a.ndim)
    mod = pl.BlockSpec((1, 1, D), lambda b, i: (b, 0, 0))
    out_blk = pl.BlockSpec((1, tm, W), lambda b, i: (b, i, 0))
    return pl.pallas_call(
        _rkv_kernel,
        grid=(B, T // tm),
        in_specs=[pl.BlockSpec((1, tm, D), lambda b, i: (b, i, 0)),
                  halo(lambda i: jnp.maximum(i * (tm // HALO) - 1, 0)),
                  halo(lambda i: jnp.minimum((i + 1) * (tm // HALO), nb - 1)),
                  pl.BlockSpec((1, D), lambda b, i: (0, 0)), mod, mod,
                  pl.BlockSpec((D, 3 * W), lambda b, i: (0, 0), pipeline_mode=pl.Buffered(1)),
                  full(p["conv"]), full(p["k_k"]), full(p["ones_bd"])],
        out_specs=[out_blk] * 4 + [pl.BlockSpec((1, tm, D), lambda b, i: (b, i, 0))],
        out_shape=[jax.ShapeDtypeStruct((B, T, W), BF16)] * 4 + [jax.ShapeDtypeStruct((B, T, D), BF16)],
        compiler_params=_cparams(("parallel", "parallel")),
        name="norm_inproj_rkv",
    )(x, x, x, g, shift, scale, p["w_in"], p["conv"], p["k_k"], p["ones_bd"])


def _lora_kernel(xn_ref, w_ref, r_ref, k_ref, v_ref, dbias, dup, ibias, iup, gup,
                 ka, rk, ones_ref, up_o, a0_o, a1_o, lw0_o, lw1_o, g_o, bonus_o, lo_ref):
    PW = up_o.shape[2]

    @pl.when(pl.program_id(1) == 0)
    def _():
        lo_ref[...] = jnp.zeros_like(lo_ref)

    lo = lo_ref[...]
    xn = xn_ref[0]
    step = GATE_LORA
    chunks = iter(range(0, w_ref.shape[1], step))

    def project():
        c = next(chunks)
        u = _dot(xn, w_ref[:, c:c + step])
        if c < PW:
            up_o[0, :, c:c + step] = u
        else:
            lo_ref[:, c - PW:c - PW + step] = u

    project()
    w_lo = jnp.tanh(lo[:, 0:LORA_PAD]).astype(BF16)
    a_lo = lo[:, LORA_PAD:2 * LORA_PAD].astype(BF16)
    g_lo = _sigmoid(lo[:, 2 * LORA_PAD:2 * LORA_PAD + GATE_LORA]).astype(BF16)
    a_sum = None
    for d, (a_o, lw_o) in enumerate(((a0_o, lw0_o), (a1_o, lw1_o))):
        project()
        z = dbias[d:d + 1, :] + _dot(w_lo, dup[d])
        lw_o[0] = -DECAY_SCALE * _sigmoid(z)
        a = _sigmoid(ibias[d:d + 1, :] + _dot(a_lo, iup[d]))
        a_o[0] = a.astype(a_o.dtype)
        a_sum = a if a_sum is None else a_sum + a
    project()
    g_o[0] = _dot(g_lo, gup[...]).astype(g_o.dtype)
    project()
    r, k, v = (z[0].astype(F32) for z in (r_ref, k_ref, v_ref))
    kpair = k * (2.0 + (a_sum - 2.0) * ka[...])
    bonus_o[0] = (_head_sum(r * kpair * rk[...], ones_ref[...]) * v).astype(bonus_o.dtype)
    project()
    assert next(chunks, None) is None


def _lora_call(xn, p, r, k, v, tm):
    B, T, D = xn.shape
    W = p["k_k"].shape[1]
    PW = p["pool_scale"].shape[1]
    NL = p["w_in"].shape[1] - 3 * W
    tm = _tile(T, tm)
    nt = T // tm
    full = lambda a: pl.BlockSpec(a.shape, lambda b, i: (0,) * a.ndim)
    consts = [p["dbias"], p["dup"], p["ibias"], p["iup"], p["gup"], p["k_a"], p["r_k"], p["ones_bd"]]
    head = lambda n: pl.BlockSpec((1, tm, n), lambda b, i: (b, jnp.minimum(i, nt - 1), 0))
    tail = lambda n: pl.BlockSpec((1, tm, n), lambda b, i: (b, jnp.maximum(i - 1, 0), 0))
    sds = lambda n, dt: jax.ShapeDtypeStruct((B, T, n), dt)
    return pl.pallas_call(
        _lora_kernel,
        grid=(B, nt + 1),
        in_specs=[head(D),
                  pl.BlockSpec((D, NL), lambda b, i: (0, 3 * W // NL), pipeline_mode=pl.Buffered(1)),
                  tail(W), tail(W), tail(W)] + [full(a) for a in consts],
        out_specs=[head(PW)] + [tail(W)] * 6,
        out_shape=[sds(PW, F32), sds(W, BF16), sds(W, BF16), sds(W, F32), sds(W, F32), sds(W, BF16),
                   sds(W, BF16)],
        scratch_shapes=[pltpu.VMEM((tm, NL - PW), F32)],
        compiler_params=_cparams(("parallel", "arbitrary")),
        name="inproj_lora_pool",
    )(xn, p["w_in"], r, k, v, *consts)


def _scan_masks(C):
    t = np.arange(C)[:, None]
    j = np.arange(2 * C)[None, :] % C
    fwd = [t > j, t >= j]
    rev = [t < j, t <= j]
    s = 1
    while s < C:
        blk = (t // (2 * s)) == (j // (2 * s))
        fwd.append(blk & ((t % (2 * s)) >= s) & ((j % (2 * s)) < s))
        rev.append(blk & ((t % (2 * s)) < s) & ((j % (2 * s)) >= s))
        s *= 2
    return np.stack(fwd + rev).astype(np.float32)


def _chunk_group(chains, masks, nlev, m0, m1, eye):
    n = len(chains)
    C = chains[0][0].shape[0]
    P = 2 * C
    stack = lambda xb: jnp.concatenate([xb * m0, xb * m1], axis=0)
    AR, BK, Vs = [], [], []
    for (At, Rt, Bt, Kt, v, _, _, _) in chains:
        AR.append(jnp.concatenate([At, Rt], axis=0).astype(BF16))
        BK.append(jnp.concatenate([stack(Bt.astype(BF16)), stack(Kt.astype(BF16))], axis=0))
        Vs.append(stack(v.astype(BF16)))
    sc = [_dot_nt(AR[i], BK[i]).astype(BF16) for i in range(n)]
    x1 = [_dot_nt(AR[i], chains[i][6].astype(BF16)) for i in range(n)]
    Nab = [sc[i][:C, :P] * masks[chains[i][7]] for i in range(n)]
    T = [eye + (Nab[i] * masks[chains[i][7] + 2]).astype(F32) for i in range(n)]
    for lv in range(1, nlev):
        Tb = [T[i].astype(BF16) for i in range(n)]
        Nl = [stack(Nab[i] * masks[chains[i][7] + 2 + lv]) for i in range(n)]
        TN = [_dot(Tb[i], Nl[i]).astype(BF16) for i in range(n)]
        T = [T[i] + _dot(TN[i], stack(Tb[i])) for i in range(n)]
    NV = [_dot(sc[i][:C, P:] * masks[chains[i][7]], Vs[i]) for i in range(n)]
    Us = [_dot(T[i].astype(BF16), stack((x1[i][:C] + NV[i]).astype(BF16))) for i in range(n)]
    UV = [jnp.concatenate([stack(Us[i].astype(BF16)), Vs[i]], axis=0) for i in range(n)]
    out = []
    for i in range(n):
        incl = masks[chains[i][7] + 1]
        Pr = jnp.concatenate([sc[i][C:, :P] * incl, sc[i][C:, P:] * incl], axis=1)
        y = x1[i][C:] + _dot(Pr, UV[i])
        Zn = chains[i][5] * (chains[i][6] + _dot_tn(UV[i], BK[i]))
        out.append((y, Zn))
    return out


def _scan_kernel(rf, kf, vf, kkf, af, lwf, rb, kb, vb, kkb, ab, lwb, s0f, s0b, ka_ref, tri_ref, m_ref,
                 yf_o, yb_o, sf_o, sb_o, *, C, G, nchunk):
    @pl.when(pl.program_id(2) == 0)
    def _():
        sf_o[...] = s0f[...]
        sb_o[...] = s0b[...]

    nlev = int(np.log2(C))
    nmask = 2 + nlev
    lane = lax.broadcasted_iota(jnp.int32, (1, LANES), 1)
    m0 = (lane < HEAD).astype(BF16)
    m1 = (lane >= HEAD).astype(BF16)
    ri = lax.broadcasted_iota(jnp.int32, (C, 2 * C), 0)
    ci = lax.broadcasted_iota(jnp.int32, (C, 2 * C), 1)
    eye = (ri == jnp.bitwise_and(ci, C - 1)).astype(F32)
    ka = ka_ref[...]
    dirs = ((rf, kf, vf, kkf, af, lwf, yf_o, sf_o, False), (rb, kb, vb, kkb, ab, lwb, yb_o, sb_o, True))
    for cidx in range(nchunk):
        chains, dests = [], []
        for (r_r, k_r, v_r, kk_r, a_r, lw_r, y_o, s_o, rev) in dirs:
            c = (nchunk - 1 - cidx) if rev else cidx
            rows = slice(c * C, (c + 1) * C)
            r, k, kk, a = (z[0, rows, :].astype(F32) for z in (r_r, k_r, kk_r, a_r))
            v, lw = v_r[0, rows, :], lw_r[0, rows, :]
            hi, lo = _split(lw)
            tri = tri_ref[1 if rev else 0]
            cum = _dot(tri, hi) + _dot(tri, lo)
            ea = jnp.exp(cum - lw)
            er = jnp.exp(cum)
            einv = jnp.exp(-cum)
            At = -(kk * ea)
            Rt = r * er
            Bt = (kk * a) * einv
            Kt = (k * (1.0 + (a - 1.0) * ka)) * einv
            gam = er[0:1, :] if rev else er[C - 1:C, :]
            for g in range(G):
                sl = slice(g * LANES, (g + 1) * LANES)
                chains.append((At[:, sl], Rt[:, sl], Bt[:, sl], Kt[:, sl], v[:, sl], gam[:, sl],
                               s_o[0, g], nmask if rev else 0))
            dests.append((y_o, s_o, rows))
        res = _chunk_group(chains, m_ref, nlev, m0, m1, eye)
        for d, (y_o, s_o, rows) in enumerate(dests):
            part = res[d * G:(d + 1) * G]
            for g in range(G):
                s_o[0, g] = part[g][1]
            y = part[0][0] if G == 1 else jnp.concatenate([q[0] for q in part], axis=1)
            y_o[0, rows, :] = y.astype(y_o.dtype)


def _scan_call(pre, a0, a1, lw0, lw1, s0f, s0b, ka, C, G, TB):
    r, k, v, kk = pre
    B, T, W = r.shape
    TB = _tile(T, TB)
    assert TB % C == 0 and W % (G * LANES) == 0
    NT = T // TB
    NG = W // (G * LANES)
    GW = G * LANES
    fwd = pl.BlockSpec((1, TB, GW), lambda b, g, i: (b, i, g))
    bwd = pl.BlockSpec((1, TB, GW), lambda b, g, i: (b, NT - 1 - i, g))
    st = pl.BlockSpec((1, G, LANES, LANES), lambda b, g, i: (b, g, 0, 0))
    masks = jnp.asarray(_scan_masks(C), BF16)
    tri = jnp.asarray(np.stack([np.tril(np.ones((C, C), np.float32)),
                                np.triu(np.ones((C, C), np.float32))])).astype(BF16)
    kern = functools.partial(_scan_kernel, C=C, G=G, nchunk=TB // C)
    return pl.pallas_call(
        kern,
        grid=(B, NG, NT),
        in_specs=[fwd] * 6 + [bwd] * 6 + [st, st,
                  pl.BlockSpec((1, GW), lambda b, g, i: (0, g)),
                  pl.BlockSpec(tri.shape, lambda b, g, i: (0, 0, 0)),
                  pl.BlockSpec(masks.shape, lambda b, g, i: (0, 0, 0))],
        out_specs=[fwd, bwd, st, st],
        out_shape=[jax.ShapeDtypeStruct((B, T, W), BF16)] * 2
                  + [jax.ShapeDtypeStruct(s0f.shape, F32)] * 2,
        compiler_params=_cparams(("parallel", "parallel", "arbitrary")),
        name="wkv7_scan",
    )(r, k, v, kk, a0, lw0, r, k, v, kk, a1, lw1, s0f, s0b, ka, tri, masks)


def _window_count(pos, half, n):
    return (jnp.minimum(pos + half, n) - jnp.maximum(pos - half, 0)).astype(F32)


def _pool2d_kernel(cur, prv, nxt, pw_ref, ps_ref, o_ref, col_ref, *, tp, halo, nrows):
    i = pl.program_id(1)
    has_prev = (i > 0).astype(F32)
    has_next = (i < pl.num_programs(1) - 1).astype(F32)
    PG = pw_ref.shape[1]
    shift = GRID_COLS.bit_length() - 1
    tok = lax.broadcasted_iota(jnp.int32, (tp, LANES), 0)
    col = jnp.bitwise_and(tok, GRID_COLS - 1)
    row = i * (tp // GRID_COLS) + jnp.right_shift(tok, shift)
    p = lax.broadcasted_iota(jnp.int32, (LANES, LANES), 0)
    q = lax.broadcasted_iota(jnp.int32, (LANES, LANES), 1)
    same_row = jnp.right_shift(p, shift) == jnp.right_shift(q, shift)
    dcol = jnp.bitwise_and(p, GRID_COLS - 1) - jnp.bitwise_and(q, GRID_COLS - 1)
    for gi, win in enumerate(POOL_WINS):
        half = win // 2
        cs = slice(gi * PG, (gi + 1) * PG)
        band = (same_row & (dcol <= half) & (dcol > -half)).astype(BF16)

        def colsum(x):
            hi, lo = _split(x)
            return _dot(band, hi) + _dot(band, lo)

        for c in range(halo // LANES):
            rs = slice(c * LANES, (c + 1) * LANES)
            col_ref[rs, :] = colsum(prv[0, rs, cs] * has_prev)
            rs2 = slice(halo + tp + c * LANES, halo + tp + (c + 1) * LANES)
            col_ref[rs2, :] = colsum(nxt[0, rs, cs] * has_next)
        for c in range(tp // LANES):
            rs = slice(c * LANES, (c + 1) * LANES)
            col_ref[halo + c * LANES:halo + (c + 1) * LANES, :] = colsum(cur[0, rs, cs])
        acc = None
        for dr in range(-half, half):
            part = col_ref[halo + dr * GRID_COLS:halo + dr * GRID_COLS + tp, :]
            acc = part if acc is None else acc + part
        inv = 1.0 / (_window_count(col, half, GRID_COLS) * _window_count(row, half, nrows))
        diff = acc * jnp.concatenate([inv] * (PG // LANES), axis=1) - cur[0, :, cs]
        o_ref[0, :, cs] = (_dot(diff.astype(BF16), pw_ref[gi]) * ps_ref[:, cs]).astype(o_ref.dtype)


def _pool2d_call(u, pool_w, pool_scale, tp):
    B, T, _ = u.shape
    PW = pool_scale.shape[1]
    halo = POOL_HALO_ROWS * GRID_COLS
    tp = _tile(T, tp)
    assert tp % halo == 0 and T % halo == 0
    nh = T // halo
    kern = functools.partial(_pool2d_kernel, tp=tp, halo=halo, nrows=T // GRID_COLS)
    return pl.pallas_call(
        kern,
        grid=(B, T // tp),
        in_specs=[
            pl.BlockSpec((1, tp, PW), lambda b, i: (b, i, 0)),
            pl.BlockSpec((1, halo, PW), lambda b, i: (b, jnp.maximum(i * (tp // halo) - 1, 0), 0)),
            pl.BlockSpec((1, halo, PW), lambda b, i: (b, jnp.minimum((i + 1) * (tp // halo), nh - 1), 0)),
            pl.BlockSpec(pool_w.shape, lambda b, i: (0, 0, 0)),
            pl.BlockSpec((1, PW), lambda b, i: (0, 0)),
        ],
        out_specs=pl.BlockSpec((1, tp, PW), lambda b, i: (b, i, 0)),
        out_shape=jax.ShapeDtypeStruct((B, T, PW), BF16),
        scratch_shapes=[pltpu.VMEM((tp + 2 * halo, pool_w.shape[1]), F32)],
        compiler_params=_cparams(("parallel", "parallel")),
        name="pool2d",
    )(u, u, u, pool_w, pool_scale)


def _pool1d_kernel(u_ref, pw_ref, ps_ref, o_ref):
    T = u_ref.shape[1]
    PG = pw_ref.shape[1]
    p = lax.broadcasted_iota(jnp.int32, (T, T), 0)
    q = lax.broadcasted_iota(jnp.int32, (T, T), 1)
    pos = lax.broadcasted_iota(jnp.int32, (T, PG), 0)
    for gi, win in enumerate(POOL_WINS):
        half = win // 2
        cs = slice(gi * PG, (gi + 1) * PG)
        band = ((p - q <= half) & (p - q > -half)).astype(BF16)
        x = u_ref[0, :, cs]
        hi, lo = _split(x)
        m = (_dot(band, hi) + _dot(band, lo)) * (1.0 / _window_count(pos, half, T))
        o_ref[0, :, cs] = (_dot((m - x).astype(BF16), pw_ref[gi]) * ps_ref[:, cs]).astype(o_ref.dtype)


def _pool1d_call(u, pool_w, pool_scale):
    B, T, _ = u.shape
    PW = pool_scale.shape[1]
    return pl.pallas_call(
        _pool1d_kernel,
        grid=(B,),
        in_specs=[
            pl.BlockSpec((1, T, PW), lambda b: (b, 0, 0)),
            pl.BlockSpec(pool_w.shape, lambda b: (0, 0, 0)),
            pl.BlockSpec((1, PW), lambda b: (0, 0)),
        ],
        out_specs=pl.BlockSpec((1, T, PW), lambda b: (b, 0, 0)),
        out_shape=jax.ShapeDtypeStruct((B, T, PW), BF16),
        compiler_params=_cparams(("parallel",)),
        name="pool1d",
    )(u, pool_w, pool_scale)


def _outproj_kernel(x_ref, yf, yb, bonus, g, p_ref, gnw, gnb, ones_ref, gt_ref, w_ref, gf_ref, sh_ref, sc_ref,
                    o_ref, xn_o):
    W = yf.shape[2]
    ones_bd = ones_ref[...]
    y = yf[0].astype(F32) + yb[0].astype(F32)
    mu = _head_sum(y, ones_bd) * (1.0 / HEAD)
    yc = y - mu
    var = _head_sum(yc * yc, ones_bd) * (1.0 / HEAD)
    yn = yc * lax.rsqrt(var + GN_EPS) * gnw[...] + gnb[...]
    mixed = ((yn + bonus[0].astype(F32)) * g[0].astype(F32)).astype(BF16)
    acc = _dot(mixed, w_ref[0:W, :]) + _dot(p_ref[0].astype(BF16), w_ref[W:, :])
    h = x_ref[0] + gt_ref[0] * acc
    o_ref[0] = h
    xn_o[0] = _modnorm(h, gf_ref[...], sh_ref[0], sc_ref[0]).astype(xn_o.dtype)


def _outproj_call(x, yf, yb, bonus, g, p, gnw, gnb, ones_bd, gate, w, g_ffn, shift, scale, tm):
    B, T, D = x.shape
    W, PW = yf.shape[2], p.shape[2]
    tm = _tile(T, tm)
    blk = lambda n: pl.BlockSpec((1, tm, n), lambda b, i: (b, i, 0))
    vec = pl.BlockSpec((1, W), lambda b, i: (0, 0))
    mod = pl.BlockSpec((1, 1, D), lambda b, i: (b, 0, 0))
    return pl.pallas_call(
        _outproj_kernel,
        grid=(B, T // tm),
        in_specs=[blk(D), blk(W), blk(W), blk(W), blk(W), blk(PW), vec, vec,
                  pl.BlockSpec(ones_bd.shape, lambda b, i: (0, 0)), mod,
                  pl.BlockSpec(w.shape, lambda b, i: (0, 0), pipeline_mode=pl.Buffered(1)),
                  pl.BlockSpec((1, D), lambda b, i: (0, 0)), mod, mod],
        out_specs=[blk(D), blk(D)],
        out_shape=[jax.ShapeDtypeStruct((B, T, D), F32), jax.ShapeDtypeStruct((B, T, D), BF16)],
        compiler_params=_cparams(("parallel", "parallel")),
        name="mix_outproj_residual",
    )(x, yf, yb, bonus, g, p, gnw, gnb, ones_bd, gate, w, g_ffn, shift, scale)


def _ffn_kernel(x_ref, xn_ref, gt_ref, wg_ref, wu_ref, wd_ref, fin_ref, o_ref, acc_ref, *, final):
    j = pl.program_id(2)

    @pl.when(j == 0)
    def _():
        acc_ref[...] = jnp.zeros_like(acc_ref)

    xn = xn_ref[0]
    h = _silu(_dot(xn, wg_ref[...])) * _dot(xn, wu_ref[...])
    acc_ref[...] += _dot(h.astype(BF16), wd_ref[...])

    @pl.when(j == pl.num_programs(2) - 1)
    def _():
        h = x_ref[0] + gt_ref[0] * acc_ref[...]
        if final:
            h = h * lax.rsqrt(jnp.mean(h * h, axis=-1, keepdims=True) + RMS_EPS) * fin_ref[...]
        o_ref[0] = h


def _ffn_call(x, xn, gate, wg, wu, wd, fin, final, tm, tf):
    B, T, D = x.shape
    F = wg.shape[1]
    tm = _tile(T, tm)
    return pl.pallas_call(
        functools.partial(_ffn_kernel, final=final),
        grid=(B, T // tm, F // tf),
        in_specs=[
            pl.BlockSpec((1, tm, D), lambda b, i, j: (b, i, 0)),
            pl.BlockSpec((1, tm, D), lambda b, i, j: (b, i, 0)),
            pl.BlockSpec((1, 1, D), lambda b, i, j: (b, 0, 0)),
            pl.BlockSpec((D, tf), lambda b, i, j: (0, j)),
            pl.BlockSpec((D, tf), lambda b, i, j: (0, j)),
            pl.BlockSpec((tf, D), lambda b, i, j: (j, 0)),
            pl.BlockSpec((1, D), lambda b, i, j: (0, 0)),
        ],
        out_specs=pl.BlockSpec((1, tm, D), lambda b, i, j: (b, i, 0)),
        out_shape=jax.ShapeDtypeStruct((B, T, D), F32),
        scratch_shapes=[pltpu.VMEM((tm, D), F32)],
        compiler_params=_cparams(("parallel", "parallel", "arbitrary")),
        name="swiglu_residual",
    )(x, xn, gate, wg, wu, wd, fin)


def _pad_rows(w, n):
    return jnp.pad(w, ((0, 0),) * (w.ndim - 2) + ((0, n - w.shape[-2]), (0, 0)))


def _layer_params(l, w_in, conv_rkv, decay_bias, decay_up, iclr_bias, iclr_up, gate_up, k_k, k_a, r_k,
                  gn_w, gn_b, pool_w, pool_scale, w_out, ffn_gate, ffn_up, ffn_down):
    RW = k_k.shape[1]
    wl = w_in[l]
    o_w = 3 * RW
    o_a = o_w + DECAY_LORA
    o_g = o_a + ICLR_LORA
    o_p = o_g + GATE_LORA
    padc = lambda w: jnp.pad(w, ((0, 0), (0, LORA_PAD - w.shape[1])))
    w_in_r = jnp.concatenate(
        [wl[:, :o_w], wl[:, o_p:], padc(wl[:, o_w:o_a]), padc(wl[:, o_a:o_g]), wl[:, o_g:o_p]], axis=1)
    row = lambda a: a[l].reshape(1, -1)
    return {
        "w_in": w_in_r.astype(BF16),
        "conv": conv_rkv[l],
        "dbias": decay_bias[l], "dup": _pad_rows(decay_up[l], LORA_PAD).astype(BF16),
        "ibias": iclr_bias[l], "iup": _pad_rows(iclr_up[l], LORA_PAD).astype(BF16),
        "gup": gate_up[l].astype(BF16),
        "k_k": row(k_k), "k_a": row(k_a), "r_k": row(r_k), "gn_w": row(gn_w), "gn_b": row(gn_b),
        "pool_w": pool_w[l].astype(BF16), "pool_scale": row(pool_scale),
        "w_out": w_out[l].astype(BF16),
        "wg": ffn_gate[l].astype(BF16), "wu": ffn_up[l].astype(BF16), "wd": ffn_down[l].astype(BF16),
    }


_CFG = dict(tm_in=512, scan_c=64, scan_g=8, scan_tb=256, tp_pool=2048, tm_out=512,
            tm_ffn=512, tf_ffn=512, tn_ada=1024)


def _forward(cfg, x, c, ctx, c_ctx, ada_w, ada_b, norm_mix, norm_ffn, w_in, conv_rkv, decay_bias, decay_up,
             iclr_bias, iclr_up, gate_up, k_k, k_a, r_k, gn_w, gn_b, pool_w, pool_scale, w_out,
             ffn_gate, ffn_up, ffn_down, final_norm):
    B, T, D = x.shape
    Tc = ctx.shape[1]
    L = ada_w.shape[0]
    RW = k_k.shape[1]
    npair = RW // LANES
    assert RW % LANES == 0 and B + 1 <= SUBLANES
    ones_bd = jnp.asarray(np.kron(np.eye(LANES // HEAD), np.ones((HEAD, HEAD))), BF16)
    fin = final_norm.reshape(1, D)

    cond = jnp.zeros((SUBLANES, D), F32).at[:B].set(c).at[B].set(c_ctx)
    mods = _ada_call(cond, ada_w, ada_b, cfg["tn_ada"])
    mods = mods.reshape(L, SUBLANES, 6, D)

    ctx_h = ctx
    for l in range(L):
        last = l == L - 1
        p = _layer_params(l, w_in, conv_rkv, decay_bias, decay_up, iclr_bias, iclr_up, gate_up, k_k, k_a,
                          r_k, gn_w, gn_b, pool_w, pool_scale, w_out, ffn_gate, ffn_up, ffn_down)
        p["ones_bd"] = ones_bd
        m_lat = [mods[l, :B, s].reshape(B, 1, D) for s in range(6)]
        m_ctx = [jnp.broadcast_to(mods[l, B, s].reshape(1, 1, D), (B, 1, D)) for s in range(6)]
        g_mix = norm_mix[l].reshape(1, D)
        g_ffn = norm_ffn[l].reshape(1, D)
        zero = jnp.zeros((B, npair, LANES, LANES), F32)

        def mixer(h, m, s0f, s0b):
            r, k, v, kk, hn = _rkv_call(h, g_mix, m[0], m[1], p, cfg["tm_in"])
            u, a0, a1, lw0, lw1, g, bonus = _lora_call(hn, p, r, k, v, cfg["tm_in"])
            yf, yb, sf, sb = _scan_call((r, k, v, kk), a0, a1, lw0, lw1, s0f, s0b, p["k_a"],
                                        cfg["scan_c"], cfg["scan_g"], cfg["scan_tb"])
            return u, (yf, yb, bonus, g), sf, sb

        def residuals(h, m, y, pooled, final):
            h, hn = _outproj_call(h, *y, pooled, p["gn_w"], p["gn_b"], ones_bd, m[2], p["w_out"],
                                  g_ffn, m[3], m[4], cfg["tm_out"])
            return _ffn_call(h, hn, m[5], p["wg"], p["wu"], p["wd"], fin, final, cfg["tm_ffn"], cfg["tf_ffn"])

        u_ctx, y_ctx, s_fwd, s_bwd = mixer(ctx_h, m_ctx, zero, zero)
        u_lat, y_lat, _, _ = mixer(x, m_lat, s_fwd, s_bwd)
        p_lat = _pool2d_call(u_lat, p["pool_w"], p["pool_scale"], cfg["tp_pool"])
        x = residuals(x, m_lat, y_lat, p_lat, last)
        if not last:
            p_ctx = _pool1d_call(u_ctx, p["pool_w"], p["pool_scale"])
            flat = lambda a: a.reshape(1, B * Tc, a.shape[2])
            ctx_h = residuals(flat(ctx_h), [mm[:1] for mm in m_ctx], tuple(flat(a) for a in y_ctx),
                              flat(p_ctx), False).reshape(B, Tc, D)
    return x


def kernel(x, c, ctx, c_ctx, ada_w, ada_b, norm_mix, norm_ffn, w_in, conv_rkv, decay_bias, decay_up,
           iclr_bias, iclr_up, gate_up, k_k, k_a, r_k, gn_w, gn_b, pool_w, pool_scale, w_out,
           ffn_gate, ffn_up, ffn_down, final_norm):
    return _forward(_CFG, x, c, ctx, c_ctx, ada_w, ada_b, norm_mix, norm_ffn, w_in, conv_rkv, decay_bias,
                    decay_up, iclr_bias, iclr_up, gate_up, k_k, k_a, r_k, gn_w, gn_b, pool_w, pool_scale,
                    w_out, ffn_gate, ffn_up, ffn_down, final_norm)
```

```python
import functools

import numpy as np
import jax
import jax.numpy as jnp
from jax import lax
from jax.experimental import pallas as pl
from jax.experimental.pallas import tpu as pltpu

F32 = jnp.float32
BF16 = jnp.bfloat16

HEAD = 64
LANES = 128
SUBLANES = 8
GRID_COLS = 64
POOL_WINS = (2, 4, 8, 16)
POOL_HALO_ROWS = 8
RMS_EPS = 1e-6
GN_EPS = 64e-5
KK_NORM_FLOOR = 1e-24
DECAY_SCALE = float(np.exp(-0.5))
DECAY_LORA = 96
ICLR_LORA = 96
GATE_LORA = 256
LORA_PAD = 128
VMEM_LIMIT = 56 * 1024 * 1024


def _cparams(sem):
    return pltpu.CompilerParams(dimension_semantics=sem, vmem_limit_bytes=VMEM_LIMIT)


def _tile(n, t):
    t = min(t, n)
    assert n % t == 0, (n, t)
    return t


def _dot(a, b):
    return jnp.dot(a, b, preferred_element_type=F32)


def _dot_nt(a, b):
    return lax.dot_general(a, b, (((1,), (1,)), ((), ())), preferred_element_type=F32)


def _dot_tn(a, b):
    return lax.dot_general(a, b, (((0,), (0,)), ((), ())), preferred_element_type=F32)


def _split(x):
    hi = x.astype(BF16)
    lo = (x - hi.astype(F32)).astype(BF16)
    return hi, lo


def _head_sum(x, ones_bd):
    xb = x.astype(BF16)
    parts = [_dot(xb[:, c * LANES:(c + 1) * LANES], ones_bd) for c in range(x.shape[1] // LANES)]
    return parts[0] if len(parts) == 1 else jnp.concatenate(parts, axis=1)


def _sigmoid(x):
    return 1.0 / (1.0 + jnp.exp(-x))


def _silu(x):
    return x * _sigmoid(x)


def _ada_kernel(c_ref, w_ref, b_ref, o_ref):
    c = c_ref[...]
    o_ref[0] = _dot(_silu(c).astype(BF16), w_ref[0].astype(BF16)) + b_ref[0]


def _ada_call(cond, ada_w, ada_b, tn):
    L, D, N = ada_w.shape
    R = cond.shape[0]
    return pl.pallas_call(
        _ada_kernel,
        grid=(L, N // tn),
        in_specs=[
            pl.BlockSpec((R, D), lambda l, j: (0, 0)),
            pl.BlockSpec((1, D, tn), lambda l, j: (l, 0, j)),
            pl.BlockSpec((1, 1, tn), lambda l, j: (l, 0, j)),
        ],
        out_specs=pl.BlockSpec((1, R, tn), lambda l, j: (l, 0, j)),
        out_shape=jax.ShapeDtypeStruct((L, R, N), F32),
        compiler_params=_cparams(("parallel", "parallel")),
        name="ada_modulation",
    )(cond, ada_w, ada_b.reshape(L, 1, N))


def _modnorm(x, g, shift, scale):
    ms = jnp.mean(x * x, axis=-1, keepdims=True)
    return x * lax.rsqrt(ms + RMS_EPS) * (g * (1.0 + scale)) + shift


HALO = SUBLANES


def _rkv_kernel(x_ref, xp_ref, xn_ref, g_ref, sh_ref, sc_ref, w_ref, cw, kkw, ones_ref,
                r_o, k_o, v_o, kk_o, xn_o):
    i = pl.program_id(1)
    has_prev = (i > 0).astype(F32)
    has_next = (i < pl.num_programs(1) - 1).astype(F32)
    tm = x_ref.shape[1]
    W = r_o.shape[2]
    norm = lambda x: _modnorm(x, g_ref[...], sh_ref[0], sc_ref[0])
    xc = norm(x_ref[0])
    xn_o[0] = xc.astype(xn_o.dtype)
    xn = jnp.concatenate([norm(xp_ref[0]) * has_prev, xc, norm(xn_ref[0]) * has_next], axis=0)
    xn = xn.astype(BF16)
    rows = slice(HALO, HALO + tm)

    def conv_proj(c0):
        u = _dot(xn, w_ref[:, c0:c0 + W])
        w = cw[:, c0:c0 + W]
        return (pltpu.roll(u, 1, 0)[rows] * w[0:1, :] + u[rows] * w[1:2, :]
                + pltpu.roll(u, tm + 2 * HALO - 1, 0)[rows] * w[2:3, :])

    r_o[0] = conv_proj(0).astype(r_o.dtype)
    k = conv_proj(W)
    k_o[0] = k.astype(k_o.dtype)
    v_o[0] = conv_proj(2 * W).astype(v_o.dtype)
    kk = k * kkw[...]
    kk = kk * lax.rsqrt(jnp.maximum(_head_sum(kk * kk, ones_ref[...]), KK_NORM_FLOOR))
    kk_o[0] = kk.astype(kk_o.dtype)


def _rkv_call(x, g, shift, scale, p, tm):
    B, T, D = x.shape
    W = p["k_k"].shape[1]
    tm = _tile(T, tm)
    nb = T // HALO
    halo = lambda f: pl.BlockSpec((1, HALO, D), lambda b, i: (b, f(i), 0))
    full = lambda a: pl.BlockSpec(a.shape, lambda b, i: (0,) * a.ndim)
    mod = pl.BlockSpec((1, 1, D), lambda b, i: (b, 0, 0))
    out_blk = pl.BlockSpec((1, tm, W), lambda b, i: (b, i, 0))
    return pl.pallas_call(
        _rkv_kernel,
        grid=(B, T // tm),
        in_specs=[pl.BlockSpec((1, tm, D), lambda b, i: (b, i, 0)),
                  halo(lambda i: jnp.maximum(i * (tm // HALO) - 1, 0)),
                  halo(lambda i: jnp.minimum((i + 1) * (tm // HALO), nb - 1)),
                  pl.BlockSpec((1, D), lambda b, i: (0, 0)), mod, mod,
                  pl.BlockSpec((D, 3 * W), lambda b, i: (0, 0), pipeline_mode=pl.Buffered(1)),
                  full(p["conv"]), full(p["k_k"]), full(p["ones_bd"])],
        out_specs=[out_blk] * 4 + [pl.BlockSpec((1, tm, D), lambda b, i: (b, i, 0))],
        out_shape=[jax.ShapeDtypeStruct((B, T, W), BF16)] * 4 + [jax.ShapeDtypeStruct((B, T, D), BF16)],
        compiler_params=_cparams(("parallel", "parallel")),
        name="norm_inproj_rkv",
    )(x, x, x, g, shift, scale, p["w_in"], p["conv"], p["k_k"], p["ones_bd"])


def _lora_kernel(xn_ref, w_ref, r_ref, k_ref, v_ref, dbias, dup, ibias, iup, gup,
                 ka, rk, ones_ref, up_o, a0_o, a1_o, lw0_o, lw1_o, g_o, bonus_o, lo_ref):
    PW = up_o.shape[2]

    @pl.when(pl.program_id(1) == 0)
    def _():
        lo_ref[...] = jnp.zeros_like(lo_ref)

    lo = lo_ref[...]
    xn = xn_ref[0]
    step = GATE_LORA
    chunks = iter(range(0, w_ref.shape[1], step))

    def project():
        c = next(chunks)
        u = _dot(xn, w_ref[:, c:c + step])
        if c < PW:
            up_o[0, :, c:c + step] = u
        else:
            lo_ref[:, c - PW:c - PW + step] = u

    project()
    w_lo = jnp.tanh(lo[:, 0:LORA_PAD]).astype(BF16)
    a_lo = lo[:, LORA_PAD:2 * LORA_PAD].astype(BF16)
    g_lo = _sigmoid(lo[:, 2 * LORA_PAD:2 * LORA_PAD + GATE_LORA]).astype(BF16)
    a_sum = None
    for d, (a_o, lw_o) in enumerate(((a0_o, lw0_o), (a1_o, lw1_o))):
        project()
        z = dbias[d:d + 1, :] + _dot(w_lo, dup[d])
        lw_o[0] = -DECAY_SCALE * _sigmoid(z)
        a = _sigmoid(ibias[d:d + 1, :] + _dot(a_lo, iup[d]))
        a_o[0] = a.astype(a_o.dtype)
        a_sum = a if a_sum is None else a_sum + a
    project()
    g_o[0] = _dot(g_lo, gup[...]).astype(g_o.dtype)
    project()
    r, k, v = (z[0].astype(F32) for z in (r_ref, k_ref, v_ref))
    kpair = k * (2.0 + (a_sum - 2.0) * ka[...])
    bonus_o[0] = (_head_sum(r * kpair * rk[...], ones_ref[...]) * v).astype(bonus_o.dtype)
    project()
    assert next(chunks, None) is None


def _lora_call(xn, p, r, k, v, tm):
    B, T, D = xn.shape
    W = p["k_k"].shape[1]
    PW = p["pool_scale"].shape[1]
    NL = p["w_in"].shape[1] - 3 * W
    tm = _tile(T, tm)
    nt = T // tm
    full = lambda a: pl.BlockSpec(a.shape, lambda b, i: (0,) * a.ndim)
    consts = [p["dbias"], p["dup"], p["ibias"], p["iup"], p["gup"], p["k_a"], p["r_k"], p["ones_bd"]]
    head = lambda n: pl.BlockSpec((1, tm, n), lambda b, i: (b, jnp.minimum(i, nt - 1), 0))
    tail = lambda n: pl.BlockSpec((1, tm, n), lambda b, i: (b, jnp.maximum(i - 1, 0), 0))
    sds = lambda n, dt: jax.ShapeDtypeStruct((B, T, n), dt)
    return pl.pallas_call(
        _lora_kernel,
        grid=(B, nt + 1),
        in_specs=[head(D),
                  pl.BlockSpec((D, NL), lambda b, i: (0, 3 * W // NL), pipeline_mode=pl.Buffered(1)),
                  tail(W), tail(W), tail(W)] + [full(a) for a in consts],
        out_specs=[head(PW)] + [tail(W)] * 6,
        out_shape=[sds(PW, F32), sds(W, BF16), sds(W, BF16), sds(W, F32), sds(W, F32), sds(W, BF16),
                   sds(W, BF16)],
        scratch_shapes=[pltpu.VMEM((tm, NL - PW), F32)],
        compiler_params=_cparams(("parallel", "arbitrary")),
        name="inproj_lora_pool",
    )(xn, p["w_in"], r, k, v, *consts)


def _scan_masks(C):
    t = np.arange(C)[:, None]
    j = np.arange(2 * C)[None, :] % C
    fwd = [t > j, t >= j]
    rev = [t < j, t <= j]
    s = 1
    while s < C:
        blk = (t // (2 * s)) == (j // (2 * s))
        fwd.append(blk & ((t % (2 * s)) >= s) & ((j % (2 * s)) < s))
        rev.append(blk & ((t % (2 * s)) < s) & ((j % (2 * s)) >= s))
        s *= 2
    return np.stack(fwd + rev).astype(np.float32)


def _chunk_group(chains, masks, nlev, m0, m1, eye):
    n = len(chains)
    C = chains[0][0].shape[0]
    P = 2 * C
    stack = lambda xb: jnp.concatenate([xb * m0, xb * m1], axis=0)
    AR, BK, Vs = [], [], []
    for (At, Rt, Bt, Kt, v, _, _, _) in chains:
        AR.append(jnp.concatenate([At, Rt], axis=0).astype(BF16))
        BK.append(jnp.concatenate([stack(Bt.astype(BF16)), stack(Kt.astype(BF16))], axis=0))
        Vs.append(stack(v.astype(BF16)))
    sc = [_dot_nt(AR[i], BK[i]).astype(BF16) for i in range(n)]
    x1 = [_dot_nt(AR[i], chains[i][6].astype(BF16)) for i in range(n)]
    Nab = [sc[i][:C, :P] * masks[chains[i][7]] for i in range(n)]
    T = [eye + (Nab[i] * masks[chains[i][7] + 2]).astype(F32) for i in range(n)]
    for lv in range(1, nlev):
        Tb = [T[i].astype(BF16) for i in range(n)]
        Nl = [stack(Nab[i] * masks[chains[i][7] + 2 + lv]) for i in range(n)]
        TN = [_dot(Tb[i], Nl[i]).astype(BF16) for i in range(n)]
        T = [T[i] + _dot(TN[i], stack(Tb[i])) for i in range(n)]
    NV = [_dot(sc[i][:C, P:] * masks[chains[i][7]], Vs[i]) for i in range(n)]
    Us = [_dot(T[i].astype(BF16), stack((x1[i][:C] + NV[i]).astype(BF16))) for i in range(n)]
    UV = [jnp.concatenate([stack(Us[i].astype(BF16)), Vs[i]], axis=0) for i in range(n)]
    out = []
    for i in range(n):
        incl = masks[chains[i][7] + 1]
        Pr = jnp.concatenate([sc[i][C:, :P] * incl, sc[i][C:, P:] * incl], axis=1)
        y = x1[i][C:] + _dot(Pr, UV[i])
        Zn = chains[i][5] * (chains[i][6] + _dot_tn(UV[i], BK[i]))
        out.append((y, Zn))
    return out


def _scan_kernel(rf, kf, vf, kkf, af, lwf, rb, kb, vb, kkb, ab, lwb, s0f, s0b, ka_ref, tri_ref, m_ref,
                 yf_o, yb_o, sf_o, sb_o, *, C, G, nchunk):
    @pl.when(pl.program_id(2) == 0)
    def _():
        sf_o[...] = s0f[...]
        sb_o[...] = s0b[...]

    nlev = int(np.log2(C))
    nmask = 2 + nlev
    lane = lax.broadcasted_iota(jnp.int32, (1, LANES), 1)
    m0 = (lane < HEAD).astype(BF16)
    m1 = (lane >= HEAD).astype(BF16)
    ri = lax.broadcasted_iota(jnp.int32, (C, 2 * C), 0)
    ci = lax.broadcasted_iota(jnp.int32, (C, 2 * C), 1)
    eye = (ri == jnp.bitwise_and(ci, C - 1)).astype(F32)
    ka = ka_ref[...]
    dirs = ((rf, kf, vf, kkf, af, lwf, yf_o, sf_o, False), (rb, kb, vb, kkb, ab, lwb, yb_o, sb_o, True))
    for cidx in range(nchunk):
        chains, dests = [], []
        for (r_r, k_r, v_r, kk_r, a_r, lw_r, y_o, s_o, rev) in dirs:
            c = (nchunk - 1 - cidx) if rev else cidx
            rows = slice(c * C, (c + 1) * C)
            r, k, kk, a = (z[0, rows, :].astype(F32) for z in (r_r, k_r, kk_r, a_r))
            v, lw = v_r[0, rows, :], lw_r[0, rows, :]
            hi, lo = _split(lw)
            tri = tri_ref[1 if rev else 0]
            cum = _dot(tri, hi) + _dot(tri, lo)
            ea = jnp.exp(cum - lw)
            er = jnp.exp(cum)
            einv = jnp.exp(-cum)
            At = -(kk * ea)
            Rt = r * er
            Bt = (kk * a) * einv
            Kt = (k * (1.0 + (a - 1.0) * ka)) * einv
            gam = er[0:1, :] if rev else er[C - 1:C, :]
            for g in range(G):
                sl = slice(g * LANES, (g + 1) * LANES)
                chains.append((At[:, sl], Rt[:, sl], Bt[:, sl], Kt[:, sl], v[:, sl], gam[:, sl],
                               s_o[0, g], nmask if rev else 0))
            dests.append((y_o, s_o, rows))
        res = _chunk_group(chains, m_ref, nlev, m0, m1, eye)
        for d, (y_o, s_o, rows) in enumerate(dests):
            part = res[d * G:(d + 1) * G]
            for g in range(G):
                s_o[0, g] = part[g][1]
            y = part[0][0] if G == 1 else jnp.concatenate([q[0] for q in part], axis=1)
            y_o[0, rows, :] = y.astype(y_o.dtype)


def _scan_call(pre, a0, a1, lw0, lw1, s0f, s0b, ka, C, G, TB):
    r, k, v, kk = pre
    B, T, W = r.shape
    TB = _tile(T, TB)
    assert TB % C == 0 and W % (G * LANES) == 0
    NT = T // TB
    NG = W // (G * LANES)
    GW = G * LANES
    fwd = pl.BlockSpec((1, TB, GW), lambda b, g, i: (b, i, g))
    bwd = pl.BlockSpec((1, TB, GW), lambda b, g, i: (b, NT - 1 - i, g))
    st = pl.BlockSpec((1, G, LANES, LANES), lambda b, g, i: (b, g, 0, 0))
    masks = jnp.asarray(_scan_masks(C), BF16)
    tri = jnp.asarray(np.stack([np.tril(np.ones((C, C), np.float32)),
                                np.triu(np.ones((C, C), np.float32))])).astype(BF16)
    kern = functools.partial(_scan_kernel, C=C, G=G, nchunk=TB // C)
    return pl.pallas_call(
        kern,
        grid=(B, NG, NT),
        in_specs=[fwd] * 6 + [bwd] * 6 + [st, st,
                  pl.BlockSpec((1, GW), lambda b, g, i: (0, g)),
                  pl.BlockSpec(tri.shape, lambda b, g, i: (0, 0, 0)),
                  pl.BlockSpec(masks.shape, lambda b, g, i: (0, 0, 0))],
        out_specs=[fwd, bwd, st, st],
        out_shape=[jax.ShapeDtypeStruct((B, T, W), BF16)] * 2
                  + [jax.ShapeDtypeStruct(s0f.shape, F32)] * 2,
        compiler_params=_cparams(("parallel", "parallel", "arbitrary")),
        name="wkv7_scan",
    )(r, k, v, kk, a0, lw0, r, k, v, kk, a1, lw1, s0f, s0b, ka, tri, masks)


def _window_count(pos, half, n):
    return (jnp.minimum(pos + half, n) - jnp.maximum(pos - half, 0)).astype(F32)


def _pool2d_kernel(cur, prv, nxt, pw_ref, ps_ref, o_ref, col_ref, *, tp, halo, nrows):
    i = pl.program_id(1)
    has_prev = (i > 0).astype(F32)
    has_next = (i < pl.num_programs(1) - 1).astype(F32)
    PG = pw_ref.shape[1]
    shift = GRID_COLS.bit_length() - 1
    tok = lax.broadcasted_iota(jnp.int32, (tp, LANES), 0)
    col = jnp.bitwise_and(tok, GRID_COLS - 1)
    row = i * (tp // GRID_COLS) + jnp.right_shift(tok, shift)
    p = lax.broadcasted_iota(jnp.int32, (LANES, LANES), 0)
    q = lax.broadcasted_iota(jnp.int32, (LANES, LANES), 1)
    same_row = jnp.right_shift(p, shift) == jnp.right_shift(q, shift)
    dcol = jnp.bitwise_and(p, GRID_COLS - 1) - jnp.bitwise_and(q, GRID_COLS - 1)
    for gi, win in enumerate(POOL_WINS):
        half = win // 2
        cs = slice(gi * PG, (gi + 1) * PG)
        band = (same_row & (dcol <= half) & (dcol > -half)).astype(BF16)

        def colsum(x):
            hi, lo = _split(x)
            return _dot(band, hi) + _dot(band, lo)

        for c in range(halo // LANES):
            rs = slice(c * LANES, (c + 1) * LANES)
            col_ref[rs, :] = colsum(prv[0, rs, cs] * has_prev)
            rs2 = slice(halo + tp + c * LANES, halo + tp + (c + 1) * LANES)
            col_ref[rs2, :] = colsum(nxt[0, rs, cs] * has_next)
        for c in range(tp // LANES):
            rs = slice(c * LANES, (c + 1) * LANES)
            col_ref[halo + c * LANES:halo + (c + 1) * LANES, :] = colsum(cur[0, rs, cs])
        acc = None
        for dr in range(-half, half):
            part = col_ref[halo + dr * GRID_COLS:halo + dr * GRID_COLS + tp, :]
            acc = part if acc is None else acc + part
        inv = 1.0 / (_window_count(col, half, GRID_COLS) * _window_count(row, half, nrows))
        diff = acc * jnp.concatenate([inv] * (PG // LANES), axis=1) - cur[0, :, cs]
        o_ref[0, :, cs] = (_dot(diff.astype(BF16), pw_ref[gi]) * ps_ref[:, cs]).astype(o_ref.dtype)


def _pool2d_call(u, pool_w, pool_scale, tp):
    B, T, _ = u.shape
    PW = pool_scale.shape[1]
    halo = POOL_HALO_ROWS * GRID_COLS
    tp = _tile(T, tp)
    assert tp % halo == 0 and T % halo == 0
    nh = T // halo
    kern = functools.partial(_pool2d_kernel, tp=tp, halo=halo, nrows=T // GRID_COLS)
    return pl.pallas_call(
        kern,
        grid=(B, T // tp),
        in_specs=[
            pl.BlockSpec((1, tp, PW), lambda b, i: (b, i, 0)),
            pl.BlockSpec((1, halo, PW), lambda b, i: (b, jnp.maximum(i * (tp // halo) - 1, 0), 0)),
            pl.BlockSpec((1, halo, PW), lambda b, i: (b, jnp.minimum((i + 1) * (tp // halo), nh - 1), 0)),
            pl.BlockSpec(pool_w.shape, lambda b, i: (0, 0, 0)),
            pl.BlockSpec((1, PW), lambda b, i: (0, 0)),
        ],
        out_specs=pl.BlockSpec((1, tp, PW), lambda b, i: (b, i, 0)),
        out_shape=jax.ShapeDtypeStruct((B, T, PW), BF16),
        scratch_shapes=[pltpu.VMEM((tp + 2 * halo, pool_w.shape[1]), F32)],
        compiler_params=_cparams(("parallel", "parallel")),
        name="pool2d",
    )(u, u, u, pool_w, pool_scale)


def _pool1d_kernel(u_ref, pw_ref, ps_ref, o_ref):
    T = u_ref.shape[1]
    PG = pw_ref.shape[1]
    p = lax.broadcasted_iota(jnp.int32, (T, T), 0)
    q = lax.broadcasted_iota(jnp.int32, (T, T), 1)
    pos = lax.broadcasted_iota(jnp.int32, (T, PG), 0)
    for gi, win in enumerate(POOL_WINS):
        half = win // 2
        cs = slice(gi * PG, (gi + 1) * PG)
        band = ((p - q <= half) & (p - q > -half)).astype(BF16)
        x = u_ref[0, :, cs]
        hi, lo = _split(x)
        m = (_dot(band, hi) + _dot(band, lo)) * (1.0 / _window_count(pos, half, T))
        o_ref[0, :, cs] = (_dot((m - x).astype(BF16), pw_ref[gi]) * ps_ref[:, cs]).astype(o_ref.dtype)


def _pool1d_call(u, pool_w, pool_scale):
    B, T, _ = u.shape
    PW = pool_scale.shape[1]
    return pl.pallas_call(
        _pool1d_kernel,
        grid=(B,),
        in_specs=[
            pl.BlockSpec((1, T, PW), lambda b: (b, 0, 0)),
            pl.BlockSpec(pool_w.shape, lambda b: (0, 0, 0)),
            pl.BlockSpec((1, PW), lambda b: (0, 0)),
        ],
        out_specs=pl.BlockSpec((1, T, PW), lambda b: (b, 0, 0)),
        out_shape=jax.ShapeDtypeStruct((B, T, PW), BF16),
        compiler_params=_cparams(("parallel",)),
        name="pool1d",
    )(u, pool_w, pool_scale)


def _outproj_kernel(x_ref, yf, yb, bonus, g, p_ref, gnw, gnb, ones_ref, gt_ref, w_ref, gf_ref, sh_ref, sc_ref,
                    o_ref, xn_o):
    W = yf.shape[2]
    ones_bd = ones_ref[...]
    y = yf[0].astype(F32) + yb[0].astype(F32)
    mu = _head_sum(y, ones_bd) * (1.0 / HEAD)
    yc = y - mu
    var = _head_sum(yc * yc, ones_bd) * (1.0 / HEAD)
    yn = yc * lax.rsqrt(var + GN_EPS) * gnw[...] + gnb[...]
    mixed = ((yn + bonus[0].astype(F32)) * g[0].astype(F32)).astype(BF16)
    acc = _dot(mixed, w_ref[0:W, :]) + _dot(p_ref[0].astype(BF16), w_ref[W:, :])
    h = x_ref[0] + gt_ref[0] * acc
    o_ref[0] = h
    xn_o[0] = _modnorm(h, gf_ref[...], sh_ref[0], sc_ref[0]).astype(xn_o.dtype)


def _outproj_call(x, yf, yb, bonus, g, p, gnw, gnb, ones_bd, gate, w, g_ffn, shift, scale, tm):
    B, T, D = x.shape
    W, PW = yf.shape[2], p.shape[2]
    tm = _tile(T, tm)
    blk = lambda n: pl.BlockSpec((1, tm, n), lambda b, i: (b, i, 0))
    vec = pl.BlockSpec((1, W), lambda b, i: (0, 0))
    mod = pl.BlockSpec((1, 1, D), lambda b, i: (b, 0, 0))
    return pl.pallas_call(
        _outproj_kernel,
        grid=(B, T // tm),
        in_specs=[blk(D), blk(W), blk(W), blk(W), blk(W), blk(PW), vec, vec,
                  pl.BlockSpec(ones_bd.shape, lambda b, i: (0, 0)), mod,
                  pl.BlockSpec(w.shape, lambda b, i: (0, 0), pipeline_mode=pl.Buffered(1)),
                  pl.BlockSpec((1, D), lambda b, i: (0, 0)), mod, mod],
        out_specs=[blk(D), blk(D)],
        out_shape=[jax.ShapeDtypeStruct((B, T, D), F32), jax.ShapeDtypeStruct((B, T, D), BF16)],
        compiler_params=_cparams(("parallel", "parallel")),
        name="mix_outproj_residual",
    )(x, yf, yb, bonus, g, p, gnw, gnb, ones_bd, gate, w, g_ffn, shift, scale)


def _ffn_kernel(x_ref, xn_ref, gt_ref, wgu_ref, wd_ref, fin_ref, o_ref, acc_ref, *, final):
    j = pl.program_id(2)

    @pl.when(j == 0)
    def _():
        acc_ref[...] = jnp.zeros_like(acc_ref)

    tf = wd_ref.shape[0]
    gu = _dot(xn_ref[0], wgu_ref[0])
    h = _silu(gu[:, 0:tf]) * gu[:, tf:]
    acc_ref[...] += _dot(h.astype(BF16), wd_ref[...])

    @pl.when(j == pl.num_programs(2) - 1)
    def _():
        h = x_ref[0] + gt_ref[0] * acc_ref[...]
        if final:
            h = h * lax.rsqrt(jnp.mean(h * h, axis=-1, keepdims=True) + RMS_EPS) * fin_ref[...]
        o_ref[0] = h


def _ffn_call(x, xn, gate, wgu, wd, fin, final, tm):
    B, T, D = x.shape
    nf, _, tf2 = wgu.shape
    tf = tf2 // 2
    tm = _tile(T, tm)
    return pl.pallas_call(
        functools.partial(_ffn_kernel, final=final),
        grid=(B, T // tm, nf),
        in_specs=[
            pl.BlockSpec((1, tm, D), lambda b, i, j: (b, i, 0)),
            pl.BlockSpec((1, tm, D), lambda b, i, j: (b, i, 0)),
            pl.BlockSpec((1, 1, D), lambda b, i, j: (b, 0, 0)),
            pl.BlockSpec((1, D, tf2), lambda b, i, j: (j, 0, 0)),
            pl.BlockSpec((tf, D), lambda b, i, j: (j, 0)),
            pl.BlockSpec((1, D), lambda b, i, j: (0, 0)),
        ],
        out_specs=pl.BlockSpec((1, tm, D), lambda b, i, j: (b, i, 0)),
        out_shape=jax.ShapeDtypeStruct((B, T, D), F32),
        scratch_shapes=[pltpu.VMEM((tm, D), F32)],
        compiler_params=_cparams(("parallel", "parallel", "arbitrary")),
        name="swiglu_residual",
    )(x, xn, gate, wgu, wd, fin)


def _pad_rows(w, n):
    return jnp.pad(w, ((0, 0),) * (w.ndim - 2) + ((0, n - w.shape[-2]), (0, 0)))


def _ffn_tiles(wg, wu, tf):
    D, F = wg.shape
    assert F % tf == 0
    tiles = lambda w: w.astype(BF16).reshape(D, F // tf, tf).transpose(1, 0, 2)
    return jnp.concatenate([tiles(wg), tiles(wu)], axis=2)


def _layer_params(l, tf, w_in, conv_rkv, decay_bias, decay_up, iclr_bias, iclr_up, gate_up, k_k, k_a, r_k,
                  gn_w, gn_b, pool_w, pool_scale, w_out, ffn_gate, ffn_up, ffn_down):
    RW = k_k.shape[1]
    wl = w_in[l]
    o_w = 3 * RW
    o_a = o_w + DECAY_LORA
    o_g = o_a + ICLR_LORA
    o_p = o_g + GATE_LORA
    padc = lambda w: jnp.pad(w, ((0, 0), (0, LORA_PAD - w.shape[1])))
    w_in_r = jnp.concatenate(
        [wl[:, :o_w], wl[:, o_p:], padc(wl[:, o_w:o_a]), padc(wl[:, o_a:o_g]), wl[:, o_g:o_p]], axis=1)
    row = lambda a: a[l].reshape(1, -1)
    return {
        "w_in": w_in_r.astype(BF16),
        "conv": conv_rkv[l],
        "dbias": decay_bias[l], "dup": _pad_rows(decay_up[l], LORA_PAD).astype(BF16),
        "ibias": iclr_bias[l], "iup": _pad_rows(iclr_up[l], LORA_PAD).astype(BF16),
        "gup": gate_up[l].astype(BF16),
        "k_k": row(k_k), "k_a": row(k_a), "r_k": row(r_k), "gn_w": row(gn_w), "gn_b": row(gn_b),
        "pool_w": pool_w[l].astype(BF16), "pool_scale": row(pool_scale),
        "w_out": w_out[l].astype(BF16),
        "wgu": _ffn_tiles(ffn_gate[l], ffn_up[l], tf), "wd": ffn_down[l].astype(BF16),
    }


_CFG = dict(tm_in=512, scan_c=64, scan_g=8, scan_tb=256, tp_pool=2048, tm_out=512,
            tm_ffn=512, tf_ffn=512, tn_ada=1024)


def _forward(cfg, x, c, ctx, c_ctx, ada_w, ada_b, norm_mix, norm_ffn, w_in, conv_rkv, decay_bias, decay_up,
             iclr_bias, iclr_up, gate_up, k_k, k_a, r_k, gn_w, gn_b, pool_w, pool_scale, w_out,
             ffn_gate, ffn_up, ffn_down, final_norm):
    B, T, D = x.shape
    Tc = ctx.shape[1]
    L = ada_w.shape[0]
    RW = k_k.shape[1]
    npair = RW // LANES
    assert RW % LANES == 0 and B + 1 <= SUBLANES
    ones_bd = jnp.asarray(np.kron(np.eye(LANES // HEAD), np.ones((HEAD, HEAD))), BF16)
    fin = final_norm.reshape(1, D)

    cond = jnp.zeros((SUBLANES, D), F32).at[:B].set(c).at[B].set(c_ctx)
    mods = _ada_call(cond, ada_w, ada_b, cfg["tn_ada"])
    mods = mods.reshape(L, SUBLANES, 6, D)

    ctx_h = ctx
    for l in range(L):
        last = l == L - 1
        p = _layer_params(l, cfg["tf_ffn"], w_in, conv_rkv, decay_bias, decay_up, iclr_bias, iclr_up, gate_up, k_k, k_a,
                          r_k, gn_w, gn_b, pool_w, pool_scale, w_out, ffn_gate, ffn_up, ffn_down)
        p["ones_bd"] = ones_bd
        m_lat = [mods[l, :B, s].reshape(B, 1, D) for s in range(6)]
        m_ctx = [jnp.broadcast_to(mods[l, B, s].reshape(1, 1, D), (B, 1, D)) for s in range(6)]
        g_mix = norm_mix[l].reshape(1, D)
        g_ffn = norm_ffn[l].reshape(1, D)
        zero = jnp.zeros((B, npair, LANES, LANES), F32)

        def mixer(h, m, s0f, s0b):
            r, k, v, kk, hn = _rkv_call(h, g_mix, m[0], m[1], p, cfg["tm_in"])
            u, a0, a1, lw0, lw1, g, bonus = _lora_call(hn, p, r, k, v, cfg["tm_in"])
            yf, yb, sf, sb = _scan_call((r, k, v, kk), a0, a1, lw0, lw1, s0f, s0b, p["k_a"],
                                        cfg["scan_c"], cfg["scan_g"], cfg["scan_tb"])
            return u, (yf, yb, bonus, g), sf, sb

        def residuals(h, m, y, pooled, final):
            h, hn = _outproj_call(h, *y, pooled, p["gn_w"], p["gn_b"], ones_bd, m[2], p["w_out"],
                                  g_ffn, m[3], m[4], cfg["tm_out"])
            return _ffn_call(h, hn, m[5], p["wgu"], p["wd"], fin, final, cfg["tm_ffn"])

        u_ctx, y_ctx, s_fwd, s_bwd = mixer(ctx_h, m_ctx, zero, zero)
        u_lat, y_lat, _, _ = mixer(x, m_lat, s_fwd, s_bwd)
        p_lat = _pool2d_call(u_lat, p["pool_w"], p["pool_scale"], cfg["tp_pool"])
        x = residuals(x, m_lat, y_lat, p_lat, last)
        if not last:
            p_ctx = _pool1d_call(u_ctx, p["pool_w"], p["pool_scale"])
            flat = lambda a: a.reshape(1, B * Tc, a.shape[2])
            ctx_h = residuals(flat(ctx_h), [mm[:1] for mm in m_ctx], tuple(flat(a) for a in y_ctx),
                              flat(p_ctx), False).reshape(B, Tc, D)
    return x


def kernel(x, c, ctx, c_ctx, ada_w, ada_b, norm_mix, norm_ffn, w_in, conv_rkv, decay_bias, decay_up,
           iclr_bias, iclr_up, gate_up, k_k, k_a, r_k, gn_w, gn_b, pool_w, pool_scale, w_out,
           ffn_gate, ffn_up, ffn_down, final_norm):
    return _forward(_CFG, x, c, ctx, c_ctx, ada_w, ada_b, norm_mix, norm_ffn, w_in, conv_rkv, decay_bias,
                    decay_up, iclr_bias, iclr_up, gate_up, k_k, k_a, r_k, gn_w, gn_b, pool_w, pool_scale,
                    w_out, ffn_gate, ffn_up, ffn_down, final_norm)
```

```python
import functools

import numpy as np
import jax
import jax.numpy as jnp
from jax import lax
from jax.experimental import pallas as pl
from jax.experimental.pallas import tpu as pltpu

F32 = jnp.float32
BF16 = jnp.bfloat16

HEAD = 64
LANES = 128
SUBLANES = 8
GRID_COLS = 64
POOL_WINS = (2, 4, 8, 16)
POOL_HALO_ROWS = 8
RMS_EPS = 1e-6
GN_EPS = 64e-5
KK_NORM_FLOOR = 1e-24
DECAY_SCALE = float(np.exp(-0.5))
DECAY_LORA = 96
ICLR_LORA = 96
GATE_LORA = 256
LORA_PAD = 128
VMEM_LIMIT = 56 * 1024 * 1024


def _cparams(sem):
    return pltpu.CompilerParams(dimension_semantics=sem, vmem_limit_bytes=VMEM_LIMIT)


def _tile(n, t):
    t = min(t, n)
    assert n % t == 0, (n, t)
    return t


def _dot(a, b):
    return jnp.dot(a, b, preferred_element_type=F32)


def _dot_nt(a, b):
    return lax.dot_general(a, b, (((1,), (1,)), ((), ())), preferred_element_type=F32)


def _dot_tn(a, b):
    return lax.dot_general(a, b, (((0,), (0,)), ((), ())), preferred_element_type=F32)


def _split(x):
    hi = x.astype(BF16)
    lo = (x - hi.astype(F32)).astype(BF16)
    return hi, lo


def _head_sum(x, ones_bd):
    xb = x.astype(BF16)
    parts = [_dot(xb[:, c * LANES:(c + 1) * LANES], ones_bd) for c in range(x.shape[1] // LANES)]
    return parts[0] if len(parts) == 1 else jnp.concatenate(parts, axis=1)


def _sigmoid(x):
    return 1.0 / (1.0 + jnp.exp(-x))


def _silu(x):
    return x * _sigmoid(x)


def _ada_kernel(c_ref, w_ref, b_ref, o_ref):
    c = c_ref[...]
    o_ref[0] = _dot(_silu(c).astype(BF16), w_ref[0].astype(BF16)) + b_ref[0]


def _ada_call(cond, ada_w, ada_b, tn):
    L, D, N = ada_w.shape
    R = cond.shape[0]
    return pl.pallas_call(
        _ada_kernel,
        grid=(L, N // tn),
        in_specs=[
            pl.BlockSpec((R, D), lambda l, j: (0, 0)),
            pl.BlockSpec((1, D, tn), lambda l, j: (l, 0, j)),
            pl.BlockSpec((1, 1, tn), lambda l, j: (l, 0, j)),
        ],
        out_specs=pl.BlockSpec((1, R, tn), lambda l, j: (l, 0, j)),
        out_shape=jax.ShapeDtypeStruct((L, R, N), F32),
        compiler_params=_cparams(("parallel", "parallel")),
        name="ada_modulation",
    )(cond, ada_w, ada_b.reshape(L, 1, N))


def _modnorm(x, g, shift, scale):
    ms = jnp.mean(x * x, axis=-1, keepdims=True)
    return x * lax.rsqrt(ms + RMS_EPS) * (g * (1.0 + scale)) + shift


HALO = SUBLANES


def _rkv_kernel(x_ref, xp_ref, xn_ref, g_ref, sh_ref, sc_ref, w_ref, cw, kkw, ones_ref,
                r_o, k_o, v_o, kk_o, xn_o):
    i = pl.program_id(1)
    has_prev = (i > 0).astype(F32)
    has_next = (i < pl.num_programs(1) - 1).astype(F32)
    tm = x_ref.shape[1]
    W = r_o.shape[2]
    norm = lambda x: _modnorm(x, g_ref[...], sh_ref[0], sc_ref[0])
    xc = norm(x_ref[0])
    xn_o[0] = xc.astype(xn_o.dtype)
    xn = jnp.concatenate([norm(xp_ref[0]) * has_prev, xc, norm(xn_ref[0]) * has_next], axis=0)
    xn = xn.astype(BF16)
    rows = slice(HALO, HALO + tm)

    def conv_proj(c0):
        u = _dot(xn, w_ref[:, c0:c0 + W])
        w = cw[:, c0:c0 + W]
        return (pltpu.roll(u, 1, 0)[rows] * w[0:1, :] + u[rows] * w[1:2, :]
                + pltpu.roll(u, tm + 2 * HALO - 1, 0)[rows] * w[2:3, :])

    r_o[0] = conv_proj(0).astype(r_o.dtype)
    k = conv_proj(W)
    k_o[0] = k.astype(k_o.dtype)
    v_o[0] = conv_proj(2 * W).astype(v_o.dtype)
    kk = k * kkw[...]
    kk = kk * lax.rsqrt(jnp.maximum(_head_sum(kk * kk, ones_ref[...]), KK_NORM_FLOOR))
    kk_o[0] = kk.astype(kk_o.dtype)


def _rkv_call(x, g, shift, scale, p, tm):
    B, T, D = x.shape
    W = p["k_k"].shape[1]
    tm = _tile(T, tm)
    nb = T // HALO
    halo = lambda f: pl.BlockSpec((1, HALO, D), lambda b, i: (b, f(i), 0))
    full = lambda a: pl.BlockSpec(a.shape, lambda b, i: (0,) * a.ndim)
    mod = pl.BlockSpec((1, 1, D), lambda b, i: (b, 0, 0))
    out_blk = pl.BlockSpec((1, tm, W), lambda b, i: (b, i, 0))
    return pl.pallas_call(
        _rkv_kernel,
        grid=(B, T // tm),
        in_specs=[pl.BlockSpec((1, tm, D), lambda b, i: (b, i, 0)),
                  halo(lambda i: jnp.maximum(i * (tm // HALO) - 1, 0)),
                  halo(lambda i: jnp.minimum((i + 1) * (tm // HALO), nb - 1)),
                  pl.BlockSpec((1, D), lambda b, i: (0, 0)), mod, mod,
                  pl.BlockSpec((D, 3 * W), lambda b, i: (0, 0), pipeline_mode=pl.Buffered(1)),
                  full(p["conv"]), full(p["k_k"]), full(p["ones_bd"])],
        out_specs=[out_blk] * 4 + [pl.BlockSpec((1, tm, D), lambda b, i: (b, i, 0))],
        out_shape=[jax.ShapeDtypeStruct((B, T, W), BF16)] * 4 + [jax.ShapeDtypeStruct((B, T, D), BF16)],
        compiler_params=_cparams(("parallel", "parallel")),
        name="norm_inproj_rkv",
    )(x, x, x, g, shift, scale, p["w_in"], p["conv"], p["k_k"], p["ones_bd"])


def _lora_kernel(xn_ref, w_ref, r_ref, k_ref, v_ref, dbias, dup, ibias, iup, gup,
                 ka, rk, ones_ref, up_o, a0_o, a1_o, lw0_o, lw1_o, g_o, bonus_o, lo_ref):
    PW = up_o.shape[2]

    @pl.when(pl.program_id(1) == 0)
    def _():
        lo_ref[...] = jnp.zeros_like(lo_ref)

    lo = lo_ref[...]
    xn = xn_ref[0]
    step = GATE_LORA
    chunks = iter(range(0, w_ref.shape[1], step))

    def project():
        c = next(chunks)
        u = _dot(xn, w_ref[:, c:c + step])
        if c < PW:
            up_o[0, :, c:c + step] = u
        else:
            lo_ref[:, c - PW:c - PW + step] = u

    project()
    w_lo = jnp.tanh(lo[:, 0:LORA_PAD]).astype(BF16)
    a_lo = lo[:, LORA_PAD:2 * LORA_PAD].astype(BF16)
    g_lo = _sigmoid(lo[:, 2 * LORA_PAD:2 * LORA_PAD + GATE_LORA]).astype(BF16)
    a_sum = None
    for d, (a_o, lw_o) in enumerate(((a0_o, lw0_o), (a1_o, lw1_o))):
        project()
        z = dbias[d:d + 1, :] + _dot(w_lo, dup[d])
        lw_o[0] = -DECAY_SCALE * _sigmoid(z)
        a = _sigmoid(ibias[d:d + 1, :] + _dot(a_lo, iup[d]))
        a_o[0] = a.astype(a_o.dtype)
        a_sum = a if a_sum is None else a_sum + a
    project()
    g_o[0] = _dot(g_lo, gup[...]).astype(g_o.dtype)
    project()
    r, k, v = (z[0].astype(F32) for z in (r_ref, k_ref, v_ref))
    kpair = k * (2.0 + (a_sum - 2.0) * ka[...])
    bonus_o[0] = (_head_sum(r * kpair * rk[...], ones_ref[...]) * v).astype(bonus_o.dtype)
    project()
    assert next(chunks, None) is None


def _lora_call(xn, p, r, k, v, tm):
    B, T, D = xn.shape
    W = p["k_k"].shape[1]
    PW = p["pool_scale"].shape[1]
    NL = p["w_in"].shape[1] - 3 * W
    tm = _tile(T, tm)
    nt = T // tm
    full = lambda a: pl.BlockSpec(a.shape, lambda b, i: (0,) * a.ndim)
    consts = [p["dbias"], p["dup"], p["ibias"], p["iup"], p["gup"], p["k_a"], p["r_k"], p["ones_bd"]]
    head = lambda n: pl.BlockSpec((1, tm, n), lambda b, i: (b, jnp.minimum(i, nt - 1), 0))
    tail = lambda n: pl.BlockSpec((1, tm, n), lambda b, i: (b, jnp.maximum(i - 1, 0), 0))
    sds = lambda n, dt: jax.ShapeDtypeStruct((B, T, n), dt)
    return pl.pallas_call(
        _lora_kernel,
        grid=(B, nt + 1),
        in_specs=[head(D),
                  pl.BlockSpec((D, NL), lambda b, i: (0, 3 * W // NL), pipeline_mode=pl.Buffered(1)),
                  tail(W), tail(W), tail(W)] + [full(a) for a in consts],
        out_specs=[head(PW)] + [tail(W)] * 6,
        out_shape=[sds(PW, F32), sds(W, BF16), sds(W, BF16), sds(W, F32), sds(W, F32), sds(W, BF16),
                   sds(W, BF16)],
        scratch_shapes=[pltpu.VMEM((tm, NL - PW), F32)],
        compiler_params=_cparams(("parallel", "arbitrary")),
        name="inproj_lora_pool",
    )(xn, p["w_in"], r, k, v, *consts)


def _scan_masks(C):
    t = np.arange(C)[:, None]
    j = np.arange(2 * C)[None, :] % C
    fwd = [t > j, t >= j]
    rev = [t < j, t <= j]
    s = 1
    while s < C:
        blk = (t // (2 * s)) == (j // (2 * s))
        fwd.append(blk & ((t % (2 * s)) >= s) & ((j % (2 * s)) < s))
        rev.append(blk & ((t % (2 * s)) < s) & ((j % (2 * s)) >= s))
        s *= 2
    return np.stack(fwd + rev).astype(np.float32)


def _chunk_group(chains, masks, nlev, m0, m1, eye):
    n = len(chains)
    C = chains[0][0].shape[0]
    P = 2 * C
    stack = lambda xb: jnp.concatenate([xb * m0, xb * m1], axis=0)
    AR, BK, Vs = [], [], []
    for (At, Rt, Bt, Kt, v, _, _, _) in chains:
        AR.append(jnp.concatenate([At, Rt], axis=0).astype(BF16))
        BK.append(jnp.concatenate([stack(Bt.astype(BF16)), stack(Kt.astype(BF16))], axis=0))
        Vs.append(stack(v.astype(BF16)))
    sc = [_dot_nt(AR[i], BK[i]).astype(BF16) for i in range(n)]
    x1 = [_dot_nt(AR[i], chains[i][6].astype(BF16)) for i in range(n)]
    Nab = [sc[i][:C, :P] * masks[chains[i][7]] for i in range(n)]
    T = [eye + (Nab[i] * masks[chains[i][7] + 2]).astype(F32) for i in range(n)]
    for lv in range(1, nlev):
        Tb = [T[i].astype(BF16) for i in range(n)]
        Nl = [stack(Nab[i] * masks[chains[i][7] + 2 + lv]) for i in range(n)]
        TN = [_dot(Tb[i], Nl[i]).astype(BF16) for i in range(n)]
        T = [T[i] + _dot(TN[i], stack(Tb[i])) for i in range(n)]
    NV = [_dot(sc[i][:C, P:] * masks[chains[i][7]], Vs[i]) for i in range(n)]
    Us = [_dot(T[i].astype(BF16), stack((x1[i][:C] + NV[i]).astype(BF16))) for i in range(n)]
    UV = [jnp.concatenate([stack(Us[i].astype(BF16)), Vs[i]], axis=0) for i in range(n)]
    out = []
    for i in range(n):
        incl = masks[chains[i][7] + 1]
        Pr = jnp.concatenate([sc[i][C:, :P] * incl, sc[i][C:, P:] * incl], axis=1)
        y = x1[i][C:] + _dot(Pr, UV[i])
        Zn = chains[i][5] * (chains[i][6] + _dot_tn(UV[i], BK[i]))
        out.append((y, Zn))
    return out


def _scan_kernel(rf, kf, vf, kkf, af, lwf, rb, kb, vb, kkb, ab, lwb, s0f, s0b, ka_ref, tri_ref, m_ref,
                 yf_o, yb_o, sf_o, sb_o, *, C, G, nchunk):
    @pl.when(pl.program_id(2) == 0)
    def _():
        sf_o[...] = s0f[...]
        sb_o[...] = s0b[...]

    nlev = int(np.log2(C))
    nmask = 2 + nlev
    lane = lax.broadcasted_iota(jnp.int32, (1, LANES), 1)
    m0 = (lane < HEAD).astype(BF16)
    m1 = (lane >= HEAD).astype(BF16)
    ri = lax.broadcasted_iota(jnp.int32, (C, 2 * C), 0)
    ci = lax.broadcasted_iota(jnp.int32, (C, 2 * C), 1)
    eye = (ri == jnp.bitwise_and(ci, C - 1)).astype(F32)
    ka = ka_ref[...]
    dirs = ((rf, kf, vf, kkf, af, lwf, yf_o, sf_o, False), (rb, kb, vb, kkb, ab, lwb, yb_o, sb_o, True))
    for cidx in range(nchunk):
        chains, dests = [], []
        for (r_r, k_r, v_r, kk_r, a_r, lw_r, y_o, s_o, rev) in dirs:
            c = (nchunk - 1 - cidx) if rev else cidx
            rows = slice(c * C, (c + 1) * C)
            r, k, kk, a = (z[0, rows, :].astype(F32) for z in (r_r, k_r, kk_r, a_r))
            v, lw = v_r[0, rows, :], lw_r[0, rows, :]
            hi, lo = _split(lw)
            tri = tri_ref[1 if rev else 0]
            cum = _dot(tri, hi) + _dot(tri, lo)
            ea = jnp.exp(cum - lw)
            er = jnp.exp(cum)
            einv = jnp.exp(-cum)
            At = -(kk * ea)
            Rt = r * er
            Bt = (kk * a) * einv
            Kt = (k * (1.0 + (a - 1.0) * ka)) * einv
            gam = er[0:1, :] if rev else er[C - 1:C, :]
            for g in range(G):
                sl = slice(g * LANES, (g + 1) * LANES)
                chains.append((At[:, sl], Rt[:, sl], Bt[:, sl], Kt[:, sl], v[:, sl], gam[:, sl],
                               s_o[0, g], nmask if rev else 0))
            dests.append((y_o, s_o, rows))
        res = _chunk_group(chains, m_ref, nlev, m0, m1, eye)
        for d, (y_o, s_o, rows) in enumerate(dests):
            part = res[d * G:(d + 1) * G]
            for g in range(G):
                s_o[0, g] = part[g][1]
            y = part[0][0] if G == 1 else jnp.concatenate([q[0] for q in part], axis=1)
            y_o[0, rows, :] = y.astype(y_o.dtype)


def _scan_call(pre, a0, a1, lw0, lw1, s0f, s0b, ka, C, G, TB):
    r, k, v, kk = pre
    B, T, W = r.shape
    TB = _tile(T, TB)
    assert TB % C == 0 and W % (G * LANES) == 0
    NT = T // TB
    NG = W // (G * LANES)
    GW = G * LANES
    fwd = pl.BlockSpec((1, TB, GW), lambda b, g, i: (b, i, g))
    bwd = pl.BlockSpec((1, TB, GW), lambda b, g, i: (b, NT - 1 - i, g))
    st = pl.BlockSpec((1, G, LANES, LANES), lambda b, g, i: (b, g, 0, 0))
    masks = jnp.asarray(_scan_masks(C), BF16)
    tri = jnp.asarray(np.stack([np.tril(np.ones((C, C), np.float32)),
                                np.triu(np.ones((C, C), np.float32))])).astype(BF16)
    kern = functools.partial(_scan_kernel, C=C, G=G, nchunk=TB // C)
    return pl.pallas_call(
        kern,
        grid=(B, NG, NT),
        in_specs=[fwd] * 6 + [bwd] * 6 + [st, st,
                  pl.BlockSpec((1, GW), lambda b, g, i: (0, g)),
                  pl.BlockSpec(tri.shape, lambda b, g, i: (0, 0, 0)),
                  pl.BlockSpec(masks.shape, lambda b, g, i: (0, 0, 0))],
        out_specs=[fwd, bwd, st, st],
        out_shape=[jax.ShapeDtypeStruct((B, T, W), BF16)] * 2
                  + [jax.ShapeDtypeStruct(s0f.shape, F32)] * 2,
        compiler_params=_cparams(("parallel", "parallel", "arbitrary")),
        name="wkv7_scan",
    )(r, k, v, kk, a0, lw0, r, k, v, kk, a1, lw1, s0f, s0b, ka, tri, masks)


def _window_count(pos, half, n):
    return (jnp.minimum(pos + half, n) - jnp.maximum(pos - half, 0)).astype(F32)


def _pool2d_kernel(cur, prv, nxt, pw_ref, ps_ref, o_ref, col_ref, *, tp, halo, nrows):
    i = pl.program_id(1)
    has_prev = (i > 0).astype(F32)
    has_next = (i < pl.num_programs(1) - 1).astype(F32)
    PG = pw_ref.shape[1]
    shift = GRID_COLS.bit_length() - 1
    tok = lax.broadcasted_iota(jnp.int32, (tp, LANES), 0)
    col = jnp.bitwise_and(tok, GRID_COLS - 1)
    row = i * (tp // GRID_COLS) + jnp.right_shift(tok, shift)
    p = lax.broadcasted_iota(jnp.int32, (LANES, LANES), 0)
    q = lax.broadcasted_iota(jnp.int32, (LANES, LANES), 1)
    same_row = jnp.right_shift(p, shift) == jnp.right_shift(q, shift)
    dcol = jnp.bitwise_and(p, GRID_COLS - 1) - jnp.bitwise_and(q, GRID_COLS - 1)
    for gi, win in enumerate(POOL_WINS):
        half = win // 2
        cs = slice(gi * PG, (gi + 1) * PG)
        band = (same_row & (dcol <= half) & (dcol > -half)).astype(BF16)

        def colsum(x):
            hi, lo = _split(x)
            return _dot(band, hi) + _dot(band, lo)

        for c in range(halo // LANES):
            rs = slice(c * LANES, (c + 1) * LANES)
            col_ref[rs, :] = colsum(prv[0, rs, cs] * has_prev)
            rs2 = slice(halo + tp + c * LANES, halo + tp + (c + 1) * LANES)
            col_ref[rs2, :] = colsum(nxt[0, rs, cs] * has_next)
        for c in range(tp // LANES):
            rs = slice(c * LANES, (c + 1) * LANES)
            col_ref[halo + c * LANES:halo + (c + 1) * LANES, :] = colsum(cur[0, rs, cs])
        acc = None
        for dr in range(-half, half):
            part = col_ref[halo + dr * GRID_COLS:halo + dr * GRID_COLS + tp, :]
            acc = part if acc is None else acc + part
        inv = 1.0 / (_window_count(col, half, GRID_COLS) * _window_count(row, half, nrows))
        diff = acc * jnp.concatenate([inv] * (PG // LANES), axis=1) - cur[0, :, cs]
        o_ref[0, :, cs] = (_dot(diff.astype(BF16), pw_ref[gi]) * ps_ref[:, cs]).astype(o_ref.dtype)


def _pool2d_call(u, pool_w, pool_scale, tp):
    B, T, _ = u.shape
    PW = pool_scale.shape[1]
    halo = POOL_HALO_ROWS * GRID_COLS
    tp = _tile(T, tp)
    assert tp % halo == 0 and T % halo == 0
    nh = T // halo
    kern = functools.partial(_pool2d_kernel, tp=tp, halo=halo, nrows=T // GRID_COLS)
    return pl.pallas_call(
        kern,
        grid=(B, T // tp),
        in_specs=[
            pl.BlockSpec((1, tp, PW), lambda b, i: (b, i, 0)),
            pl.BlockSpec((1, halo, PW), lambda b, i: (b, jnp.maximum(i * (tp // halo) - 1, 0), 0)),
            pl.BlockSpec((1, halo, PW), lambda b, i: (b, jnp.minimum((i + 1) * (tp // halo), nh - 1), 0)),
            pl.BlockSpec(pool_w.shape, lambda b, i: (0, 0, 0)),
            pl.BlockSpec((1, PW), lambda b, i: (0, 0)),
        ],
        out_specs=pl.BlockSpec((1, tp, PW), lambda b, i: (b, i, 0)),
        out_shape=jax.ShapeDtypeStruct((B, T, PW), BF16),
        scratch_shapes=[pltpu.VMEM((tp + 2 * halo, pool_w.shape[1]), F32)],
        compiler_params=_cparams(("parallel", "parallel")),
        name="pool2d",
    )(u, u, u, pool_w, pool_scale)


def _pool1d_kernel(u_ref, pw_ref, ps_ref, o_ref):
    T = u_ref.shape[1]
    PG = pw_ref.shape[1]
    p = lax.broadcasted_iota(jnp.int32, (T, T), 0)
    q = lax.broadcasted_iota(jnp.int32, (T, T), 1)
    pos = lax.broadcasted_iota(jnp.int32, (T, PG), 0)
    for gi, win in enumerate(POOL_WINS):
        half = win // 2
        cs = slice(gi * PG, (gi + 1) * PG)
        band = ((p - q <= half) & (p - q > -half)).astype(BF16)
        x = u_ref[0, :, cs]
        hi, lo = _split(x)
        m = (_dot(band, hi) + _dot(band, lo)) * (1.0 / _window_count(pos, half, T))
        o_ref[0, :, cs] = (_dot((m - x).astype(BF16), pw_ref[gi]) * ps_ref[:, cs]).astype(o_ref.dtype)


def _pool1d_call(u, pool_w, pool_scale):
    B, T, _ = u.shape
    PW = pool_scale.shape[1]
    return pl.pallas_call(
        _pool1d_kernel,
        grid=(B,),
        in_specs=[
            pl.BlockSpec((1, T, PW), lambda b: (b, 0, 0)),
            pl.BlockSpec(pool_w.shape, lambda b: (0, 0, 0)),
            pl.BlockSpec((1, PW), lambda b: (0, 0)),
        ],
        out_specs=pl.BlockSpec((1, T, PW), lambda b: (b, 0, 0)),
        out_shape=jax.ShapeDtypeStruct((B, T, PW), BF16),
        compiler_params=_cparams(("parallel",)),
        name="pool1d",
    )(u, pool_w, pool_scale)


OUT_CHUNK = 512


def _outproj_kernel(x_ref, yf, yb, bonus, g, p_ref, gnw, gnb, ones_ref, gt_ref, w_ref, gf_ref, sh_ref, sc_ref,
                    o_ref, xn_o):
    W = yf.shape[2]
    ones_bd = ones_ref[...]
    y = yf[0].astype(F32) + yb[0].astype(F32)
    mu = _head_sum(y, ones_bd) * (1.0 / HEAD)
    yc = y - mu
    var = _head_sum(yc * yc, ones_bd) * (1.0 / HEAD)
    yn = yc * lax.rsqrt(var + GN_EPS) * gnw[...] + gnb[...]
    mixed = ((yn + bonus[0].astype(F32)) * g[0].astype(F32)).astype(BF16)
    pooled = p_ref[0].astype(BF16)
    D = o_ref.shape[2]
    ss = None
    for c in range(0, D, OUT_CHUNK):
        cs = slice(c, c + OUT_CHUNK)
        acc = _dot(mixed, w_ref[0:W, cs]) + _dot(pooled, w_ref[W:, cs])
        h = x_ref[0, :, cs] + gt_ref[0, :, cs] * acc
        o_ref[0, :, cs] = h
        s = jnp.sum(h * h, axis=-1, keepdims=True)
        ss = s if ss is None else ss + s
    scale = lax.rsqrt(ss * (1.0 / D) + RMS_EPS)
    gain = gf_ref[...] * (1.0 + sc_ref[0])
    for c in range(0, D, OUT_CHUNK):
        cs = slice(c, c + OUT_CHUNK)
        xn_o[0, :, cs] = (o_ref[0, :, cs] * scale * gain[:, cs] + sh_ref[0, :, cs]).astype(xn_o.dtype)


def _outproj_call(x, yf, yb, bonus, g, p, gnw, gnb, ones_bd, gate, w, g_ffn, shift, scale, tm):
    B, T, D = x.shape
    W, PW = yf.shape[2], p.shape[2]
    tm = _tile(T, tm)
    blk = lambda n: pl.BlockSpec((1, tm, n), lambda b, i: (b, i, 0))
    vec = pl.BlockSpec((1, W), lambda b, i: (0, 0))
    mod = pl.BlockSpec((1, 1, D), lambda b, i: (b, 0, 0))
    return pl.pallas_call(
        _outproj_kernel,
        grid=(B, T // tm),
        in_specs=[blk(D), blk(W), blk(W), blk(W), blk(W), blk(PW), vec, vec,
                  pl.BlockSpec(ones_bd.shape, lambda b, i: (0, 0)), mod,
                  pl.BlockSpec(w.shape, lambda b, i: (0, 0), pipeline_mode=pl.Buffered(1)),
                  pl.BlockSpec((1, D), lambda b, i: (0, 0)), mod, mod],
        out_specs=[blk(D), blk(D)],
        out_shape=[jax.ShapeDtypeStruct((B, T, D), F32), jax.ShapeDtypeStruct((B, T, D), BF16)],
        compiler_params=_cparams(("parallel", "parallel")),
        name="mix_outproj_residual",
    )(x, yf, yb, bonus, g, p, gnw, gnb, ones_bd, gate, w, g_ffn, shift, scale)


def _ffn_kernel(x_ref, xn_ref, gt_ref, wg_ref, wu_ref, wd_ref, fin_ref, o_ref, acc_ref, *, final):
    j = pl.program_id(2)

    @pl.when(j == 0)
    def _():
        acc_ref[...] = jnp.zeros_like(acc_ref)

    xn = xn_ref[0]
    h = _silu(_dot(xn, wg_ref[...])) * _dot(xn, wu_ref[...])
    acc_ref[...] += _dot(h.astype(BF16), wd_ref[...])

    @pl.when(j == pl.num_programs(2) - 1)
    def _():
        h = x_ref[0] + gt_ref[0] * acc_ref[...]
        if final:
            h = h * lax.rsqrt(jnp.mean(h * h, axis=-1, keepdims=True) + RMS_EPS) * fin_ref[...]
        o_ref[0] = h


def _ffn_call(x, xn, gate, wg, wu, wd, fin, final, tm, tf):
    B, T, D = x.shape
    F = wg.shape[1]
    tm = _tile(T, tm)
    return pl.pallas_call(
        functools.partial(_ffn_kernel, final=final),
        grid=(B, T // tm, F // tf),
        in_specs=[
            pl.BlockSpec((1, tm, D), lambda b, i, j: (b, i, 0)),
            pl.BlockSpec((1, tm, D), lambda b, i, j: (b, i, 0)),
            pl.BlockSpec((1, 1, D), lambda b, i, j: (b, 0, 0)),
            pl.BlockSpec((D, tf), lambda b, i, j: (0, j)),
            pl.BlockSpec((D, tf), lambda b, i, j: (0, j)),
            pl.BlockSpec((tf, D), lambda b, i, j: (j, 0)),
            pl.BlockSpec((1, D), lambda b, i, j: (0, 0)),
        ],
        out_specs=pl.BlockSpec((1, tm, D), lambda b, i, j: (b, i, 0)),
        out_shape=jax.ShapeDtypeStruct((B, T, D), F32),
        scratch_shapes=[pltpu.VMEM((tm, D), F32)],
        compiler_params=_cparams(("parallel", "parallel", "arbitrary")),
        name="swiglu_residual",
    )(x, xn, gate, wg, wu, wd, fin)


def _pad_rows(w, n):
    return jnp.pad(w, ((0, 0),) * (w.ndim - 2) + ((0, n - w.shape[-2]), (0, 0)))


def _layer_params(l, w_in, conv_rkv, decay_bias, decay_up, iclr_bias, iclr_up, gate_up, k_k, k_a, r_k,
                  gn_w, gn_b, pool_w, pool_scale, w_out, ffn_gate, ffn_up, ffn_down):
    RW = k_k.shape[1]
    wl = w_in[l]
    o_w = 3 * RW
    o_a = o_w + DECAY_LORA
    o_g = o_a + ICLR_LORA
    o_p = o_g + GATE_LORA
    padc = lambda w: jnp.pad(w, ((0, 0), (0, LORA_PAD - w.shape[1])))
    w_in_r = jnp.concatenate(
        [wl[:, :o_w], wl[:, o_p:], padc(wl[:, o_w:o_a]), padc(wl[:, o_a:o_g]), wl[:, o_g:o_p]], axis=1)
    row = lambda a: a[l].reshape(1, -1)
    return {
        "w_in": w_in_r.astype(BF16),
        "conv": conv_rkv[l],
        "dbias": decay_bias[l], "dup": _pad_rows(decay_up[l], LORA_PAD).astype(BF16),
        "ibias": iclr_bias[l], "iup": _pad_rows(iclr_up[l], LORA_PAD).astype(BF16),
        "gup": gate_up[l].astype(BF16),
        "k_k": row(k_k), "k_a": row(k_a), "r_k": row(r_k), "gn_w": row(gn_w), "gn_b": row(gn_b),
        "pool_w": pool_w[l].astype(BF16), "pool_scale": row(pool_scale),
        "w_out": w_out[l].astype(BF16),
        "wg": ffn_gate[l].astype(BF16), "wu": ffn_up[l].astype(BF16), "wd": ffn_down[l].astype(BF16),
    }


_CFG = dict(tm_in=512, scan_c=64, scan_g=8, scan_tb=256, tp_pool=2048, tm_out=512,
            tm_ffn=512, tf_ffn=512, tn_ada=1024)


def _forward(cfg, x, c, ctx, c_ctx, ada_w, ada_b, norm_mix, norm_ffn, w_in, conv_rkv, decay_bias, decay_up,
             iclr_bias, iclr_up, gate_up, k_k, k_a, r_k, gn_w, gn_b, pool_w, pool_scale, w_out,
             ffn_gate, ffn_up, ffn_down, final_norm):
    B, T, D = x.shape
    Tc = ctx.shape[1]
    L = ada_w.shape[0]
    RW = k_k.shape[1]
    npair = RW // LANES
    assert RW % LANES == 0 and B + 1 <= SUBLANES
    ones_bd = jnp.asarray(np.kron(np.eye(LANES // HEAD), np.ones((HEAD, HEAD))), BF16)
    fin = final_norm.reshape(1, D)

    cond = jnp.zeros((SUBLANES, D), F32).at[:B].set(c).at[B].set(c_ctx)
    mods = _ada_call(cond, ada_w, ada_b, cfg["tn_ada"])
    mods = mods.reshape(L, SUBLANES, 6, D)

    ctx_h = ctx
    for l in range(L):
        last = l == L - 1
        p = _layer_params(l, w_in, conv_rkv, decay_bias, decay_up, iclr_bias, iclr_up, gate_up, k_k, k_a,
                          r_k, gn_w, gn_b, pool_w, pool_scale, w_out, ffn_gate, ffn_up, ffn_down)
        p["ones_bd"] = ones_bd
        m_lat = [mods[l, :B, s].reshape(B, 1, D) for s in range(6)]
        m_ctx = [jnp.broadcast_to(mods[l, B, s].reshape(1, 1, D), (B, 1, D)) for s in range(6)]
        g_mix = norm_mix[l].reshape(1, D)
        g_ffn = norm_ffn[l].reshape(1, D)
        zero = jnp.zeros((B, npair, LANES, LANES), F32)

        def mixer(h, m, s0f, s0b):
            r, k, v, kk, hn = _rkv_call(h, g_mix, m[0], m[1], p, cfg["tm_in"])
            u, a0, a1, lw0, lw1, g, bonus = _lora_call(hn, p, r, k, v, cfg["tm_in"])
            yf, yb, sf, sb = _scan_call((r, k, v, kk), a0, a1, lw0, lw1, s0f, s0b, p["k_a"],
                                        cfg["scan_c"], cfg["scan_g"], cfg["scan_tb"])
            return u, (yf, yb, bonus, g), sf, sb

        def residuals(h, m, y, pooled, final):
            h, hn = _outproj_call(h, *y, pooled, p["gn_w"], p["gn_b"], ones_bd, m[2], p["w_out"],
                                  g_ffn, m[3], m[4], cfg["tm_out"])
            return _ffn_call(h, hn, m[5], p["wg"], p["wu"], p["wd"], fin, final, cfg["tm_ffn"], cfg["tf_ffn"])

        u_ctx, y_ctx, s_fwd, s_bwd = mixer(ctx_h, m_ctx, zero, zero)
        u_lat, y_lat, _, _ = mixer(x, m_lat, s_fwd, s_bwd)
        p_lat = _pool2d_call(u_lat, p["pool_w"], p["pool_scale"], cfg["tp_pool"])
        x = residuals(x, m_lat, y_lat, p_lat, last)
        if not last:
            p_ctx = _pool1d_call(u_ctx, p["pool_w"], p["pool_scale"])
            flat = lambda a: a.reshape(1, B * Tc, a.shape[2])
            ctx_h = residuals(flat(ctx_h), [mm[:1] for mm in m_ctx], tuple(flat(a) for a in y_ctx),
                              flat(p_ctx), False).reshape(B, Tc, D)
    return x


def kernel(x, c, ctx, c_ctx, ada_w, ada_b, norm_mix, norm_ffn, w_in, conv_rkv, decay_bias, decay_up,
           iclr_bias, iclr_up, gate_up, k_k, k_a, r_k, gn_w, gn_b, pool_w, pool_scale, w_out,
           ffn_gate, ffn_up, ffn_down, final_norm):
    return _forward(_CFG, x, c, ctx, c_ctx, ada_w, ada_b, norm_mix, norm_ffn, w_in, conv_rkv, decay_bias,
                    decay_up, iclr_bias, iclr_up, gate_up, k_k, k_a, r_k, gn_w, gn_b, pool_w, pool_scale,
                    w_out, ffn_gate, ffn_up, ffn_down, final_norm)
```
